```python
import math
import jax, jax.numpy as jnp
from jax import lax
import numpy as np

D_MODEL = 1024
BATCH = 8
SEQ = 4096
DEPTH = 1
DEC_BATCH = 16
DEC_SEQ = 4096
PAST_LEN = 128

N_HEADS = 16
QK_NOPE = 64
QK_ROPE = 32
V_DIM = 64
Q_LORA = 384
KV_LORA = 256
ROPE_THETA = 10000.0
Q_BLOCK = 128
D_INNER = 2 * D_MODEL
SSD_HEADDIM = 64
SSD_HEADS = D_INNER // SSD_HEADDIM
SSD_GROUPS = 4
D_STATE = 128
CONV_WIDTH = 5
CONV_DIM = D_INNER + 2 * SSD_GROUPS * D_STATE
CHUNK = 128
DT_MIN = 0.001
DT_MAX = 0.1
N_EXPERTS = 64
TOP_K = 8
N_EXPERT_GROUPS = 8
TOPK_GROUPS = 4
D_EXPERT = D_MODEL // 4
D_SHARED = D_MODEL // 4
ROUTED_SCALE = 2.5
MOE_BLOCK = 128
LN_EPS = 1e-5
RMS_EPS = 1e-6
DN_ALPHA = (2 * DEPTH) ** 0.25
DN_BETA = (8 * DEPTH) ** -0.25
IN_SPLITS = (Q_LORA, KV_LORA, QK_ROPE, D_INNER, CONV_DIM, SSD_HEADS, SSD_HEADS, D_MODEL, D_MODEL)
D_IN_PROJ = Q_LORA + KV_LORA + QK_ROPE + D_INNER + CONV_DIM + 2 * SSD_HEADS + 2 * D_MODEL

kernel_name = "hybrid_mla_ssd_moe_encoder"


def _layer_norm(x, g, b):
    xf = x.astype(jnp.float32)
    mu = jnp.mean(xf, axis=-1, keepdims=True)
    var = jnp.mean(jnp.square(xf - mu), axis=-1, keepdims=True)
    y = (xf - mu) * lax.rsqrt(var + LN_EPS) * g.astype(jnp.float32) + b.astype(jnp.float32)
    return y.astype(x.dtype)


def _rms_norm(x, w):
    xf = x.astype(jnp.float32)
    y = xf * lax.rsqrt(jnp.mean(jnp.square(xf), axis=-1, keepdims=True) + RMS_EPS) * w.astype(jnp.float32)
    return y.astype(x.dtype)


def _rope_tables(seq_len):
    inv_freq = ROPE_THETA ** (-jnp.arange(0, QK_ROPE, 2, dtype=jnp.float32) / QK_ROPE)
    ang = jnp.arange(seq_len, dtype=jnp.float32)[:, None] * inv_freq[None, :]
    return jnp.cos(ang), jnp.sin(ang)


def _apply_rope(x, cos, sin):
    xf = x.astype(jnp.float32)
    x1, x2 = jnp.split(xf, 2, axis=-1)
    return jnp.concatenate([x1 * cos - x2 * sin, x2 * cos + x1 * sin], axis=-1).astype(x.dtype)


def _mla(c_q, c_kv, k_rope, q_norm_w, kv_norm_w, w_uq, w_ukv):
    b, s, _ = c_q.shape
    q = jnp.einsum('bsr,rhd->bshd', _rms_norm(c_q, q_norm_w), w_uq)
    kv = jnp.einsum('bsr,rhd->bshd', _rms_norm(c_kv, kv_norm_w), w_ukv)
    cos, sin = _rope_tables(s)
    q_nope = q[..., :QK_NOPE]
    q_rope = _apply_rope(q[..., QK_NOPE:], cos[:, None, :], sin[:, None, :])
    k_nope, v = kv[..., :QK_NOPE], kv[..., QK_NOPE:]
    k_rope = _apply_rope(k_rope, cos, sin)
    scale = (QK_NOPE + QK_ROPE) ** -0.5
    n_blk = s // Q_BLOCK

    def to_blocks(t):
        return jnp.moveaxis(t.reshape(b, n_blk, Q_BLOCK, *t.shape[2:]), 1, 0)

    def attend(blk):
        qn, qr = blk
        sc = (jnp.einsum('bqhd,bkhd->bhqk', qn, k_nope, preferred_element_type=jnp.float32)
              + jnp.einsum('bqhr,bkr->bhqk', qr, k_rope, preferred_element_type=jnp.float32)) * scale
        p = jax.nn.softmax(sc, axis=-1).astype(v.dtype)
        return jnp.einsum('bhqk,bkhd->bqhd', p, v)

    o = lax.map(attend, (to_blocks(q_nope), to_blocks(q_rope)))
    return jnp.moveaxis(o, 0, 1).reshape(b, s, N_HEADS * V_DIM)


def _depthwise_conv(x, w, bias):
    y = lax.conv_general_dilated(x, w[:, None, :].astype(x.dtype), window_strides=(1,),
                                 padding=[(CONV_WIDTH // 2, CONV_WIDTH // 2)],
                                 dimension_numbers=('NWC', 'WIO', 'NWC'),
                                 feature_group_count=x.shape[-1])
    return y + bias


def _ssd_scan(xh, dt, a, bm, cm):
    b, l, h, p = xh.shape
    g, n = bm.shape[2], bm.shape[3]
    r = h // g
    c = l // CHUNK
    x = (xh * dt[..., None]).reshape(b, c, CHUNK, g, r, p)
    a_cum = jnp.cumsum((dt * a).reshape(b, c, CHUNK, g, r), axis=2)
    bc = bm.reshape(b, c, CHUNK, g, n)
    cc = cm.reshape(b, c, CHUNK, g, n)
    tril = jnp.tril(jnp.ones((CHUNK, CHUNK), dtype=bool))[:, :, None, None]
    decay_qs = jnp.exp(jnp.where(tril, a_cum[:, :, :, None] - a_cum[:, :, None], -jnp.inf))
    cb = jnp.einsum('bcqgn,bcsgn->bcqsg', cc, bc)
    y_diag = jnp.einsum('bcqsgr,bcsgrp->bcqgrp', cb[..., None] * decay_qs, x)
    x_dec = x * jnp.exp(a_cum[:, :, -1:] - a_cum)[..., None]
    states = jnp.einsum('bcsgn,bcsgrp->bcgrpn', bc, x_dec)
    chunk_decay = jnp.exp(a_cum[:, :, -1])

    def step(h_prev, inp):
        st, dec = inp
        return h_prev * dec[..., None, None] + st, h_prev

    h0 = jnp.zeros((b, g, r, p, n), xh.dtype)
    _, prev = lax.scan(step, h0, (jnp.moveaxis(states, 1, 0), jnp.moveaxis(chunk_decay, 1, 0)))
    prev = jnp.moveaxis(prev, 0, 1)
    y_off = jnp.einsum('bcqgn,bcgrpn->bcqgrp', cc, prev) * jnp.exp(a_cum)[..., None]
    return (y_diag + y_off).reshape(b, l, h, p)


def _flip(t):
    return jnp.flip(t, axis=1)


def _bi_ssd(z, xbc, dt_f, dt_b, conv_w, conv_b, dt_bias_f, dt_bias_b, a_log_f, a_log_b, d_skip, norm_w):
    b, s, _ = z.shape
    f32 = jnp.float32
    xbc = jax.nn.silu(_depthwise_conv(xbc, conv_w, conv_b)).astype(f32)
    xs, bm, cm = jnp.split(xbc, [D_INNER, D_INNER + SSD_GROUPS * D_STATE], axis=-1)
    xh = xs.reshape(b, s, SSD_HEADS, SSD_HEADDIM)
    bm = bm.reshape(b, s, SSD_GROUPS, D_STATE)
    cm = cm.reshape(b, s, SSD_GROUPS, D_STATE)
    dtf = jax.nn.softplus(dt_f.astype(f32) + dt_bias_f.astype(f32))
    dtb = jax.nn.softplus(dt_b.astype(f32) + dt_bias_b.astype(f32))
    y_f = _ssd_scan(xh, dtf, -jnp.exp(a_log_f.astype(f32)), bm, cm)
    y_b = _flip(_ssd_scan(_flip(xh), _flip(dtb), -jnp.exp(a_log_b.astype(f32)), _flip(bm), _flip(cm)))
    y = y_f + y_b + d_skip.astype(f32)[:, None] * xh
    y = y.reshape(b, s, D_INNER) * jax.nn.silu(z.astype(f32))
    return _rms_norm(y, norm_w).astype(z.dtype)


def _swiglu(x, w_gate, w_up, w_down):
    return (jax.nn.silu(x @ w_gate) * (x @ w_up)) @ w_down


def _routed_experts(xf, idx, wts, w_gate, w_up, w_down):
    t, d = xf.shape
    n = t * TOP_K
    flat_e = idx.reshape(n)
    flat_tok = jnp.arange(n, dtype=jnp.int32) // TOP_K
    flat_w = wts.reshape(n)
    order = jnp.argsort(flat_e)
    sorted_e = flat_e[order]
    counts = jnp.bincount(flat_e, length=N_EXPERTS)
    padded = (counts + MOE_BLOCK - 1) // MOE_BLOCK * MOE_BLOCK
    start = jnp.cumsum(counts) - counts
    pad_end = jnp.cumsum(padded)
    pad_start = pad_end - padded
    dest = pad_start[sorted_e] + jnp.arange(n, dtype=jnp.int32) - start[sorted_e]
    n_blocks = -(-n // MOE_BLOCK) + N_EXPERTS
    rows = n_blocks * MOE_BLOCK
    row_tok = jnp.full((rows,), t, jnp.int32).at[dest].set(flat_tok[order])
    row_w = jnp.zeros((rows,), wts.dtype).at[dest].set(flat_w[order])
    block_e = jnp.minimum(jnp.searchsorted(pad_end, jnp.arange(n_blocks, dtype=jnp.int32) * MOE_BLOCK, side='right'),
                          N_EXPERTS - 1)
    x_pad = jnp.concatenate([xf, jnp.zeros((1, d), xf.dtype)], axis=0)

    def run_block(blk):
        e, tok, w = blk
        xb = x_pad[tok]
        return _swiglu(xb, w_gate[e], w_up[e], w_down[e]) * w[:, None]

    out = lax.map(run_block, (block_e, row_tok.reshape(n_blocks, MOE_BLOCK), row_w.reshape(n_blocks, MOE_BLOCK)))
    return jax.ops.segment_sum(out.reshape(rows, d), row_tok, num_segments=t + 1)[:t]


def _moe(x, w_router, router_bias, w_gate, w_up, w_down, ws_gate, ws_up, ws_down):
    b, s, d = x.shape
    t = b * s
    xf = x.reshape(t, d)
    scores = jax.nn.sigmoid(jnp.dot(xf, w_router, preferred_element_type=jnp.float32))
    choice = scores + router_bias.astype(jnp.float32)
    grp = choice.reshape(t, N_EXPERT_GROUPS, N_EXPERTS // N_EXPERT_GROUPS)
    grp_score = jnp.sum(lax.top_k(grp, 2)[0], axis=-1)
    top_g = lax.top_k(grp_score, TOPK_GROUPS)[1]
    g_mask = jnp.any(top_g[..., None] == jnp.arange(N_EXPERT_GROUPS), axis=1)
    e_mask = jnp.repeat(g_mask, N_EXPERTS // N_EXPERT_GROUPS, axis=-1)
    idx = lax.top_k(jnp.where(e_mask, choice, -jnp.inf), TOP_K)[1]
    wts = jnp.take_along_axis(scores, idx, axis=-1)
    wts = (wts / jnp.sum(wts, axis=-1, keepdims=True) * ROUTED_SCALE).astype(x.dtype)
    y = _routed_experts(xf, idx, wts, w_gate, w_up, w_down) + _swiglu(xf, ws_gate, ws_up, ws_down)
    return y.reshape(b, s, d)


def _encoder_layer(x, w_in, q_norm_w, kv_norm_w, w_uq, w_ukv, w_o_attn, conv_w, conv_b, dt_bias_f, dt_bias_b,
                   a_log_f, a_log_b, d_skip, ssd_norm_w, w_o_ssd, w_out, ln1_g, ln1_b, w_router, router_bias,
                   w_gate, w_up, w_down, ws_gate, ws_up, ws_down, ln2_g, ln2_b):
    split_points = np.cumsum(IN_SPLITS)[:-1].tolist()
    c_q, c_kv, k_rope, z, xbc, dt_f, dt_b, g_a, g_b = jnp.split(x @ w_in, split_points, axis=-1)
    attn = _mla(c_q, c_kv, k_rope, q_norm_w, kv_norm_w, w_uq, w_ukv) @ w_o_attn
    ssd = _bi_ssd(z, xbc, dt_f, dt_b, conv_w, conv_b, dt_bias_f, dt_bias_b, a_log_f, a_log_b, d_skip,
                  ssd_norm_w) @ w_o_ssd
    mixed = (jax.nn.sigmoid(g_a) * attn + jax.nn.sigmoid(g_b) * ssd) @ w_out
    x = _layer_norm(DN_ALPHA * x + mixed, ln1_g, ln1_b)
    x = _layer_norm(DN_ALPHA * x + _moe(x, w_router, router_bias, w_gate, w_up, w_down, ws_gate, ws_up, ws_down),
                    ln2_g, ln2_b)
    return x


def _trunk(x, layer_params):
    for layer in range(DEPTH):
        x = _encoder_layer(x, *[p[layer] for p in layer_params])
    return x


def setup_inputs(seed: int = 0) -> dict:
    key = jax.random.key(seed)
    ks = jax.random.split(key, 32)
    f32 = jnp.float32
    L = DEPTH

    def nrm(k, shape, scale):
        return jax.random.normal(k, shape, f32) * scale

    def gain(k, shape):
        return 1.0 + 0.01 * jax.random.normal(k, shape, f32)

    def dt_bias(k):
        dt = jnp.exp(jax.random.uniform(k, (L, SSD_HEADS), f32, math.log(DT_MIN), math.log(DT_MAX)))
        return dt + jnp.log(-jnp.expm1(-dt))

    return {
        "x_prompt": nrm(ks[0], (BATCH, SEQ, D_MODEL), 1.0),
        "x_sample": nrm(ks[1], (DEC_BATCH, DEC_SEQ, D_MODEL), 1.0),
        "w_in": nrm(ks[2], (L, D_MODEL, D_IN_PROJ), D_MODEL ** -0.5),
        "q_norm_w": gain(ks[3], (L, Q_LORA)),
        "kv_norm_w": gain(ks[4], (L, KV_LORA)),
        "w_uq": nrm(ks[5], (L, Q_LORA, N_HEADS, QK_NOPE + QK_ROPE), Q_LORA ** -0.5),
        "w_ukv": jnp.concatenate([nrm(ks[6], (L, KV_LORA, N_HEADS, QK_NOPE), KV_LORA ** -0.5),
                                  nrm(ks[7], (L, KV_LORA, N_HEADS, V_DIM), DN_BETA * KV_LORA ** -0.5)], axis=-1),
        "w_o_attn": nrm(ks[8], (L, N_HEADS * V_DIM, D_MODEL), DN_BETA * (N_HEADS * V_DIM) ** -0.5),
        "conv_w": nrm(ks[9], (L, CONV_WIDTH, CONV_DIM), CONV_WIDTH ** -0.5),
        "conv_b": nrm(ks[10], (L, CONV_DIM), 0.01),
        "dt_bias_f": dt_bias(ks[11]),
        "dt_bias_b": dt_bias(ks[12]),
        "a_log_f": jnp.log(jax.random.uniform(ks[13], (L, SSD_HEADS), f32, 1.0, 16.0)),
        "a_log_b": jnp.log(jax.random.uniform(ks[14], (L, SSD_HEADS), f32, 1.0, 16.0)),
        "d_skip": gain(ks[15], (L, SSD_HEADS)),
        "ssd_norm_w": gain(ks[16], (L, D_INNER)),
        "w_o_ssd": nrm(ks[17], (L, D_INNER, D_MODEL), DN_BETA * D_INNER ** -0.5),
        "w_out": nrm(ks[18], (L, D_MODEL, D_MODEL), DN_BETA * D_MODEL ** -0.5),
        "ln1_g": gain(ks[19], (L, D_MODEL)),
        "ln1_b": nrm(ks[20], (L, D_MODEL), 0.01),
        "w_router": nrm(ks[21], (L, D_MODEL, N_EXPERTS), D_MODEL ** -0.5),
        "router_bias": nrm(ks[22], (L, N_EXPERTS), 0.01),
        "w_gate": nrm(ks[23], (L, N_EXPERTS, D_MODEL, D_EXPERT), D_MODEL ** -0.5),
        "w_up": nrm(ks[24], (L, N_EXPERTS, D_MODEL, D_EXPERT), D_MODEL ** -0.5),
        "w_down": nrm(ks[25], (L, N_EXPERTS, D_EXPERT, D_MODEL), DN_BETA * D_EXPERT ** -0.5),
        "ws_gate": nrm(ks[26], (L, D_MODEL, D_SHARED), D_MODEL ** -0.5),
        "ws_up": nrm(ks[27], (L, D_MODEL, D_SHARED), D_MODEL ** -0.5),
        "ws_down": nrm(ks[28], (L, D_SHARED, D_MODEL), DN_BETA * D_SHARED ** -0.5),
        "ln2_g": gain(ks[29], (L, D_MODEL)),
        "ln2_b": nrm(ks[30], (L, D_MODEL), 0.01),
    }


def reference(x_prompt, x_sample, w_in, q_norm_w, kv_norm_w, w_uq, w_ukv, w_o_attn, conv_w, conv_b, dt_bias_f,
              dt_bias_b, a_log_f, a_log_b, d_skip, ssd_norm_w, w_o_ssd, w_out, ln1_g, ln1_b, w_router, router_bias,
              w_gate, w_up, w_down, ws_gate, ws_up, ws_down, ln2_g, ln2_b):
    layer_params = (w_in, q_norm_w, kv_norm_w, w_uq, w_ukv, w_o_attn, conv_w, conv_b, dt_bias_f, dt_bias_b,
                    a_log_f, a_log_b, d_skip, ssd_norm_w, w_o_ssd, w_out, ln1_g, ln1_b, w_router, router_bias,
                    w_gate, w_up, w_down, ws_gate, ws_up, ws_down, ln2_g, ln2_b)
    y_prompt = _trunk(x_prompt, layer_params)
    y_sample = _trunk(x_sample, layer_params)
    return (y_prompt, y_sample)
```

```python
import functools
import math

import jax
import jax.numpy as jnp
import numpy as np
from jax import lax
from jax.experimental import pallas as pl
from jax.experimental.pallas import tpu as pltpu

F32 = jnp.float32
BF16 = jnp.bfloat16

D_MODEL = 1024
DEPTH = 1
N_HEADS = 16
QK_NOPE = 64
QK_ROPE = 32
V_DIM = 64
Q_LORA = 384
KV_LORA = 256
ROPE_THETA = 10000.0
D_INNER = 2 * D_MODEL
SSD_HEADDIM = 64
SSD_HEADS = D_INNER // SSD_HEADDIM
SSD_GROUPS = 4
D_STATE = 128
CONV_WIDTH = 5
CONV_DIM = D_INNER + 2 * SSD_GROUPS * D_STATE
CHUNK = 128
N_EXPERTS = 64
TOP_K = 8
N_EXPERT_GROUPS = 8
TOPK_GROUPS = 4
D_EXPERT = D_MODEL // 4
D_SHARED = D_MODEL // 4
ROUTED_SCALE = 2.5
LN_EPS = 1e-5
RMS_EPS = 1e-6
DN_ALPHA = (2 * DEPTH) ** 0.25

LANES = 128
HEAD_PAD = 128
QKV_W = 768
KROPE_OFF = Q_LORA + KV_LORA
EXPERT_BLOCK = 256
VMEM_LIMIT = 56 * 1024 * 1024


def _cparams(sem, vmem=VMEM_LIMIT):
    return pltpu.CompilerParams(dimension_semantics=sem, vmem_limit_bytes=vmem)


def _dot(a, b):
    return jnp.dot(a, b, preferred_element_type=F32)


def _dot_nt(a, b):
    return lax.dot_general(a, b, (((1,), (1,)), ((), ())), preferred_element_type=F32)


def _split3(a):
    h = a.astype(BF16)
    r = a - h.astype(F32)
    m = r.astype(BF16)
    l = (r - m.astype(F32)).astype(BF16)
    return h, m, l


def _dot_split_lhs(a_f32, b_bf16):
    h, m, l = _split3(a_f32)
    return _dot(h, b_bf16) + _dot(m, b_bf16) + _dot(l, b_bf16)


def _silu(x):
    return x * jax.nn.sigmoid(x)


def _layer_norm_rows(r, g, b):
    mu = jnp.mean(r, axis=-1, keepdims=True)
    d = r - mu
    var = jnp.mean(d * d, axis=-1, keepdims=True)
    return d * lax.rsqrt(var + LN_EPS) * g + b


def _inproj_body(x_ref, wq_ref, wz_ref, wx_ref, wdt_ref, wg_ref,
                 qkv_ref, z_ref, xbc_ref, dt_ref, g_ref):
    xb = x_ref[...].astype(BF16)
    qkv_ref[...] = _dot(xb, wq_ref[...]).astype(BF16)
    z_ref[...] = _silu(_dot(xb, wz_ref[...])).astype(BF16)
    xbc_ref[...] = _dot(xb, wx_ref[...]).astype(BF16)
    dt_ref[...] = _dot(xb, wdt_ref[...])
    g_ref[...] = jax.nn.sigmoid(_dot(xb, wg_ref[...])).astype(BF16)


def _in_proj(x2d, wq, wz, wx, wdt, wg, tm=512):
    t = x2d.shape[0]
    const = lambda w: pl.BlockSpec(w.shape, lambda i: (0, 0), pipeline_mode=pl.Buffered(1))
    row = lambda n: pl.BlockSpec((tm, n), lambda i: (i, 0))
    return pl.pallas_call(
        _inproj_body,
        grid=(t // tm,),
        in_specs=[row(D_MODEL), const(wq), const(wz), const(wx), const(wdt), const(wg)],
        out_specs=[row(QKV_W), row(D_INNER), row(CONV_DIM), row(LANES), row(2 * D_MODEL)],
        out_shape=[jax.ShapeDtypeStruct((t, QKV_W), BF16),
                   jax.ShapeDtypeStruct((t, D_INNER), BF16),
                   jax.ShapeDtypeStruct((t, CONV_DIM), BF16),
                   jax.ShapeDtypeStruct((t, LANES), F32),
                   jax.ShapeDtypeStruct((t, 2 * D_MODEL), BF16)],
        compiler_params=_cparams(("parallel",)),
        name="in_proj",
    )(x2d, wq, wz, wx, wdt, wg)


def _mla_prep_body(qkv_ref, qn_ref, kvn_ref, wq_ref, wk_ref, wv_ref, cq_ref, sq_ref, tk_ref,
                   q_ref, kt_ref, v_ref):
    qkv = qkv_ref[0].astype(F32)
    cq = qkv[:, :Q_LORA]
    cq = cq * lax.rsqrt(jnp.mean(cq * cq, axis=-1, keepdims=True) + RMS_EPS) * qn_ref[...]
    ckv = qkv[:, Q_LORA:KROPE_OFF]
    ckv = ckv * lax.rsqrt(jnp.mean(ckv * ckv, axis=-1, keepdims=True) + RMS_EPS) * kvn_ref[...]
    ckv_b = ckv.astype(BF16)
    qq = _dot(cq.astype(BF16), wq_ref[...])
    half = N_HEADS * HEAD_PAD
    cq_t = cq_ref[...]
    sq_t = sq_ref[...]
    for h in range(N_HEADS):
        lo = h * HEAD_PAD
        q_ref[0, :, lo:lo + HEAD_PAD] = (qq[:, lo:lo + HEAD_PAD] * cq_t
                                          + qq[:, half + lo:half + lo + HEAD_PAD] * sq_t).astype(BF16)
    kr = (qkv[:, KROPE_OFF:] * tk_ref[...]).astype(BF16)
    e = jnp.concatenate([ckv_b, kr], axis=1)
    kt_ref[0] = _dot_nt(wk_ref[...], e).astype(BF16)
    v_ref[0] = _dot(ckv_b, wv_ref[...]).astype(BF16)


def _mla_prep(qkv, qn, kvn, wq, wk, wv, cq_t, sq_t, tk_t, ts=512):
    b, s, _ = qkv.shape
    const = lambda w: pl.BlockSpec(w.shape, lambda bi, i: (0, 0))
    tab = pl.BlockSpec((ts, LANES), lambda bi, i: (i, 0))
    return pl.pallas_call(
        _mla_prep_body,
        grid=(b, s // ts),
        in_specs=[pl.BlockSpec((1, ts, QKV_W), lambda bi, i: (bi, i, 0)),
                  const(qn), const(kvn), const(wq), const(wk), const(wv), tab, tab, tab],
        out_specs=[pl.BlockSpec((1, ts, N_HEADS * HEAD_PAD), lambda bi, i: (bi, i, 0)),
                   pl.BlockSpec((1, N_HEADS * HEAD_PAD, ts), lambda bi, i: (bi, 0, i)),
                   pl.BlockSpec((1, ts, N_HEADS * V_DIM), lambda bi, i: (bi, i, 0))],
        out_shape=[jax.ShapeDtypeStruct((b, s, N_HEADS * HEAD_PAD), BF16),
                   jax.ShapeDtypeStruct((b, N_HEADS * HEAD_PAD, s), BF16),
                   jax.ShapeDtypeStruct((b, s, N_HEADS * V_DIM), BF16)],
        compiler_params=_cparams(("parallel", "parallel")),
        name="mla_prep",
    )(qkv, qn, kvn, wq, wk, wv, cq_t, sq_t, tk_t)


def _attn_body(q_ref, kt_ref, v_ref, o_ref):
    v = v_ref[0]
    outs = []
    for hh in range(2):
        q = q_ref[0, :, hh * HEAD_PAD:(hh + 1) * HEAD_PAD]
        kt = kt_ref[0, hh * HEAD_PAD:(hh + 1) * HEAD_PAD, :]
        s = _dot(q, kt)
        m = jnp.max(s, axis=-1, keepdims=True)
        p = jnp.exp(s - m)
        l = jnp.sum(p, axis=-1, keepdims=True)
        outs.append(_dot(p.astype(BF16), v) / l)
    lane = lax.broadcasted_iota(jnp.int32, outs[0].shape, 1)
    o_ref[0] = jnp.where(lane < V_DIM, outs[0], outs[1]).astype(BF16)


def _attention(q, kt, v, tq=256):
    b, s, _ = q.shape
    pairs = N_HEADS // 2
    return pl.pallas_call(
        _attn_body,
        grid=(b, pairs, s // tq),
        in_specs=[pl.BlockSpec((1, tq, 2 * HEAD_PAD), lambda bi, j, i: (bi, i, j)),
                  pl.BlockSpec((1, 2 * HEAD_PAD, s), lambda bi, j, i: (bi, j, 0)),
                  pl.BlockSpec((1, s, 2 * V_DIM), lambda bi, j, i: (bi, 0, j))],
        out_specs=pl.BlockSpec((1, tq, 2 * V_DIM), lambda bi, j, i: (bi, i, j)),
        out_shape=jax.ShapeDtypeStruct((b, s, N_HEADS * V_DIM), BF16),
        compiler_params=_cparams(("parallel", "parallel", "arbitrary")),
        name="attention",
    )(q, kt, v)


CONV_HALO = 16


def _conv_body(prev_ref, main_ref, next_ref, w_ref, b_ref, o_ref, ext_ref):
    i = pl.program_id(1)
    n = pl.num_programs(1)
    ts = main_ref.shape[1]
    prev = prev_ref[0].astype(F32)
    nxt = next_ref[0].astype(F32)
    ext_ref[0:CONV_HALO, :] = jnp.where(i > 0, prev, 0.0)
    ext_ref[CONV_HALO:CONV_HALO + ts, :] = main_ref[0].astype(F32)
    ext_ref[CONV_HALO + ts:, :] = jnp.where(i < n - 1, nxt, 0.0)
    acc = b_ref[...]
    for k in range(CONV_WIDTH):
        off = CONV_HALO - CONV_WIDTH // 2 + k
        acc = acc + ext_ref[off:off + ts, :] * w_ref[k:k + 1, :]
    o_ref[0] = _silu(acc).astype(BF16)


def _conv(xbc, w8, bias, ts=512, tc=512):
    b, s, c = xbc.shape
    r = ts // CONV_HALO
    last = s // CONV_HALO - 1
    return pl.pallas_call(
        _conv_body,
        grid=(b, s // ts, c // tc),
        in_specs=[pl.BlockSpec((1, CONV_HALO, tc), lambda bi, i, j: (bi, jnp.maximum(i * r - 1, 0), j)),
                  pl.BlockSpec((1, ts, tc), lambda bi, i, j: (bi, i, j)),
                  pl.BlockSpec((1, CONV_HALO, tc), lambda bi, i, j: (bi, jnp.minimum((i + 1) * r, last), j)),
                  pl.BlockSpec((8, tc), lambda bi, i, j: (0, j)),
                  pl.BlockSpec((1, tc), lambda bi, i, j: (0, j))],
        out_specs=pl.BlockSpec((1, ts, tc), lambda bi, i, j: (bi, i, j)),
        out_shape=jax.ShapeDtypeStruct((b, s, c), BF16),
        scratch_shapes=[pltpu.VMEM((ts + 2 * CONV_HALO, tc), F32)],
        compiler_params=_cparams(("parallel", "parallel", "parallel")),
        name="conv",
    )(xbc, xbc, xbc, w8, bias)


def _ssd_body(reverse, xs_ref, b_ref, c_ref, dt_ref, dtb_ref, alog_ref, e_ref, y_ref, state_ref):
    @pl.when(pl.program_id(1) == 0)
    def _():
        state_ref[...] = jnp.zeros_like(state_ref)

    q = CHUNK
    lane0 = SSD_HEADS if reverse else 0
    rows = lax.broadcasted_iota(jnp.int32, (q, q), 0)
    cols = lax.broadcasted_iota(jnp.int32, (q, q), 1)
    mask = (rows <= cols) if reverse else (rows >= cols)
    tri = mask.astype(BF16)

    x = dt_ref[0] + dtb_ref[...]
    dt = jnp.maximum(x, 0.0) + jnp.log(1.0 + jnp.exp(-jnp.abs(x)))
    da = dt * (-jnp.exp(alog_ref[...]))
    cum = _dot_split_lhs_rhs(tri, da)
    cum_t = cum.T
    dt_t = dt.T
    total = cum[0:1, :] if reverse else cum[q - 1:q, :]
    w = dt * jnp.exp(total - cum)
    w_exp = _dot(w.astype(BF16), e_ref[...])
    xs = xs_ref[0]
    x_dec = (xs.astype(F32) * w_exp).astype(BF16)

    bm = b_ref[0].astype(F32)
    cm = c_ref[0]
    cm_f = cm.astype(F32)
    state_old = state_ref[...].astype(BF16)
    gw = D_STATE
    hp = SSD_HEADS // SSD_GROUPS * SSD_HEADDIM
    new_states = []
    for g in range(SSD_GROUPS):
        bt = bm[:, g * gw:(g + 1) * gw].T.astype(BF16)
        new_states.append(_dot(bt, x_dec[:, g * hp:(g + 1) * hp]))
        cb = _dot(cm[:, g * gw:(g + 1) * gw], bt)
        c_g = cm_f[:, g * gw:(g + 1) * gw]
        for jj in range(hp // LANES):
            j = g * (hp // LANES) + jj
            sl = slice(j * LANES, (j + 1) * LANES)
            rhs = jnp.concatenate([xs[:, sl], state_old[:, sl]], axis=0)
            ys = []
            for hh in range(2):
                ln = lane0 + 2 * j + hh
                col = jnp.broadcast_to(cum[:, ln:ln + 1], (q, q))
                seg = col - cum_t[ln:ln + 1, :]
                decay = jnp.exp(jnp.where(mask, seg, -jnp.inf))
                m = cb * decay * dt_t[ln:ln + 1, :]
                cs = c_g * jnp.exp(col)
                lhs = jnp.concatenate([m.astype(BF16), cs.astype(BF16)], axis=1)
                ys.append(_dot(lhs, rhs))
            lane = lax.broadcasted_iota(jnp.int32, (q, LANES), 1)
            y_ref[0, :, sl] = jnp.where(lane < SSD_HEADDIM, ys[0], ys[1]).astype(BF16)

    dec = jnp.broadcast_to(jnp.exp(total), (8, LANES))
    dec_exp = _dot_split_lhs(dec, e_ref[...])[0:1, :]
    state_ref[...] = state_ref[...] * dec_exp + jnp.concatenate(new_states, axis=1)


def _dot_split_lhs_rhs(tri_bf16, da_f32):
    h, m, l = _split3(da_f32)
    return _dot(tri_bf16, h) + _dot(tri_bf16, m) + _dot(tri_bf16, l)


def _ssd(conv_out, dt, dt_bias, a_log, expand, reverse):
    b, s, _ = conv_out.shape
    nc = s // CHUNK
    cidx = (lambda c: nc - 1 - c) if reverse else (lambda c: c)
    bcol = D_INNER // (SSD_GROUPS * D_STATE)
    gn = SSD_GROUPS * D_STATE
    const = lambda w: pl.BlockSpec(w.shape, lambda bi, c: (0, 0))
    return pl.pallas_call(
        functools.partial(_ssd_body, reverse),
        grid=(b, nc),
        in_specs=[pl.BlockSpec((1, CHUNK, D_INNER), lambda bi, c: (bi, cidx(c), 0)),
                  pl.BlockSpec((1, CHUNK, gn), lambda bi, c: (bi, cidx(c), bcol)),
                  pl.BlockSpec((1, CHUNK, gn), lambda bi, c: (bi, cidx(c), bcol + 1)),
                  pl.BlockSpec((1, CHUNK, LANES), lambda bi, c: (bi, cidx(c), 0)),
                  const(dt_bias), const(a_log), const(expand)],
        out_specs=pl.BlockSpec((1, CHUNK, D_INNER), lambda bi, c: (bi, cidx(c), 0)),
        out_shape=jax.ShapeDtypeStruct((b, s, D_INNER), BF16),
        scratch_shapes=[pltpu.VMEM((D_STATE, D_INNER), F32)],
        compiler_params=_cparams(("parallel", "arbitrary")),
        name="ssd_bwd" if reverse else "ssd_fwd",
    )(conv_out, conv_out, conv_out, dt, dt_bias, a_log, expand)


def _mix_body(x_ref, o_ref, yf_ref, yb_ref, xs_ref, z_ref, g_ref,
              woa_ref, wos_ref, wout_ref, dsk_ref, nw_ref, lg_ref, lb_ref, x1_ref):
    attn = _dot(o_ref[...], woa_ref[...])
    y = yf_ref[...].astype(F32) + yb_ref[...].astype(F32) + dsk_ref[...] * xs_ref[...].astype(F32)
    y = y * z_ref[...].astype(F32)
    yn = y * lax.rsqrt(jnp.mean(y * y, axis=-1, keepdims=True) + RMS_EPS) * nw_ref[...]
    ssd = _dot(yn.astype(BF16), wos_ref[...])
    g = g_ref[...].astype(F32)
    mixed_in = g[:, :D_MODEL] * attn + g[:, D_MODEL:] * ssd
    mixed = _dot(mixed_in.astype(BF16), wout_ref[...])
    x1_ref[...] = _layer_norm_rows(DN_ALPHA * x_ref[...] + mixed, lg_ref[...], lb_ref[...])


def _mix(x2d, o, yf, yb, conv_out, z, g, woa, wos, wout, dsk, nw, lg, lb, tm=256):
    t = x2d.shape[0]
    const = lambda w: pl.BlockSpec(w.shape, lambda i: (0, 0))
    row = lambda n: pl.BlockSpec((tm, n), lambda i: (i, 0))
    return pl.pallas_call(
        _mix_body,
        grid=(t // tm,),
        in_specs=[row(D_MODEL), row(D_MODEL), row(D_INNER), row(D_INNER), row(D_INNER), row(D_INNER),
                  row(2 * D_MODEL), const(woa), const(wos), const(wout), const(dsk), const(nw),
                  const(lg), const(lb)],
        out_specs=row(D_MODEL),
        out_shape=jax.ShapeDtypeStruct((t, D_MODEL), F32),
        compiler_params=_cparams(("parallel",)),
        name="mix_ln1",
    )(x2d, o, yf, yb, conv_out, z, g, woa, wos, wout, dsk, nw, lg, lb)


def _router_body(x_ref, wh_ref, wl_ref, bias_ref, su_ref, idx_ref, wts_ref, pos_ref, cnt_ref, carry_ref):
    @pl.when(pl.program_id(0) == 0)
    def _():
        carry_ref[...] = jnp.zeros_like(carry_ref)

    tm = x_ref.shape[0]
    x = x_ref[...]
    xh = x.astype(BF16)
    xl = (x - xh.astype(F32)).astype(BF16)
    wh = wh_ref[...]
    logits = _dot_nt(wh, xh) + _dot_nt(wh, xl) + _dot_nt(wl_ref[...], xh)
    scores = jax.nn.sigmoid(logits)
    choice = scores + bias_ref[...]
    gsz = N_EXPERTS // N_EXPERT_GROUPS
    shp = (N_EXPERT_GROUPS, gsz, tm)
    ch = choice.reshape(shp)
    sc = scores.reshape(shp)
    jio = lax.broadcasted_iota(jnp.int32, shp, 1)
    gio = lax.broadcasted_iota(jnp.int32, shp, 0)
    eio = gio * gsz + jio
    neg = -jnp.inf

    m1 = jnp.max(ch, axis=1, keepdims=True)
    i1 = jnp.min(jnp.where(ch == m1, jio, gsz), axis=1, keepdims=True)
    m2 = jnp.max(jnp.where(jio == i1, neg, ch), axis=1, keepdims=True)
    gs = m1 + m2
    gio1 = lax.broadcasted_iota(jnp.int32, (N_EXPERT_GROUPS, 1, tm), 0)
    sel = jnp.zeros(gs.shape, jnp.bool_)
    cur = gs
    for _ in range(TOPK_GROUPS):
        m = jnp.max(cur, axis=0, keepdims=True)
        gi = jnp.min(jnp.where(cur == m, gio1, N_EXPERT_GROUPS), axis=0, keepdims=True)
        hit = gio1 == gi
        sel = jnp.logical_or(sel, hit)
        cur = jnp.where(hit, neg, cur)
    masked = jnp.where(sel, ch, neg)

    def red(op, v):
        return op(op(v, axis=1, keepdims=True), axis=0, keepdims=True)

    hits, idxs, ws = [], [], []
    for _ in range(TOP_K):
        m = red(jnp.max, masked)
        ei = red(jnp.min, jnp.where(masked == m, eio, N_EXPERTS))
        hit = eio == ei
        hits.append(hit)
        idxs.append(ei)
        ws.append(red(jnp.sum, jnp.where(hit, sc, 0.0)))
        masked = jnp.where(hit, neg, masked)
    wsum = ws[0]
    for k in range(1, TOP_K):
        wsum = wsum + ws[k]

    onehot = hits[0]
    for k in range(1, TOP_K):
        onehot = jnp.logical_or(onehot, hits[k])
    oh = onehot.astype(F32).reshape(N_EXPERTS, tm)
    before = _dot(oh.astype(BF16), su_ref[...]) + carry_ref[...]
    before3 = before.reshape(shp)
    for k in range(TOP_K):
        idx_ref[k:k + 1, :] = idxs[k].reshape(1, tm)
        wts_ref[k:k + 1, :] = (ws[k] / wsum * ROUTED_SCALE).reshape(1, tm)
        pos_ref[k:k + 1, :] = red(jnp.sum, jnp.where(hits[k], before3, 0.0)).reshape(1, tm).astype(jnp.int32)
    carry_ref[...] = carry_ref[...] + jnp.sum(oh, axis=1, keepdims=True)
    cnt_ref[...] = jnp.broadcast_to(carry_ref[...], cnt_ref.shape)


def _router(x1, wr_hi, wr_lo, bias, su, tm=512):
    t = x1.shape[0]
    const = lambda w: pl.BlockSpec(w.shape, lambda i: (0, 0))
    out_tok = pl.BlockSpec((TOP_K, tm), lambda i: (0, i))
    return pl.pallas_call(
        _router_body,
        grid=(t // tm,),
        in_specs=[pl.BlockSpec((tm, D_MODEL), lambda i: (i, 0)), const(wr_hi), const(wr_lo), const(bias),
                  const(su)],
        out_specs=[out_tok, out_tok, out_tok, pl.BlockSpec((N_EXPERTS, LANES), lambda i: (0, 0))],
        out_shape=[jax.ShapeDtypeStruct((TOP_K, t), jnp.int32),
                   jax.ShapeDtypeStruct((TOP_K, t), F32),
                   jax.ShapeDtypeStruct((TOP_K, t), jnp.int32),
                   jax.ShapeDtypeStruct((N_EXPERTS, LANES), F32)],
        scratch_shapes=[pltpu.VMEM((N_EXPERTS, 1), F32)],
        compiler_params=_cparams(("arbitrary",)),
        name="router",
    )(x1, wr_hi, wr_lo, bias, su)


def _row_copy_wait(src_ref, dst_hbm, sem, n_rows, times):
    for _ in range(times):
        pltpu.make_async_copy(src_ref.at[pl.ds(0, n_rows)], dst_hbm.at[pl.ds(0, n_rows)], sem).wait()


def _scatter_body(ps_ref, pd_ref, nu_ref, idx_ref, pos_ref, x_hbm, out_hbm, zero_ref, sem, zsem):
    tm = idx_ref.shape[1]
    base = pl.program_id(0) * tm
    nb = out_hbm.shape[0] // EXPERT_BLOCK

    @pl.when(pl.program_id(0) == 0)
    def _():
        zero_ref[...] = jnp.zeros_like(zero_ref)

        def zcopy(row):
            start = pl.multiple_of(row, EXPERT_BLOCK)
            return pltpu.make_async_copy(zero_ref, out_hbm.at[pl.ds(start, EXPERT_BLOCK)], zsem)

        def last_block(e):
            return ps_ref[e] + pd_ref[e] - EXPERT_BLOCK

        def zstart(e, c):
            @pl.when(pd_ref[e] > 0)
            def _():
                zcopy(last_block(e)).start()
            return c

        def zwait(e, c):
            @pl.when(pd_ref[e] > 0)
            def _():
                zcopy(last_block(e)).wait()
            return c

        def tstart(b, c):
            zcopy(b * EXPERT_BLOCK).start()
            return c

        def twait(b, c):
            zcopy(b * EXPERT_BLOCK).wait()
            return c

        lax.fori_loop(0, N_EXPERTS, zstart, 0)
        lax.fori_loop(nu_ref[0], nb, tstart, 0)
        lax.fori_loop(0, N_EXPERTS, zwait, 0)
        lax.fori_loop(nu_ref[0], nb, twait, 0)

    def issue(t, c):
        for k in range(TOP_K):
            d = ps_ref[idx_ref[k, t]] + pos_ref[k, t]
            pltpu.make_async_copy(x_hbm.at[pl.ds(base + t, 1)], out_hbm.at[pl.ds(d, 1)], sem).start()
        return c

    lax.fori_loop(0, tm, issue, 0)
    _row_copy_wait(x_hbm, out_hbm, sem, tm, TOP_K)


def _scatter(pad_start, padded, n_used, idx, pos, x1, rows, tm=256):
    t = x1.shape[0]
    smem_tok = pl.BlockSpec((TOP_K, tm), lambda i, ps, pd, nu: (0, i), memory_space=pltpu.SMEM)
    grid_spec = pltpu.PrefetchScalarGridSpec(
        num_scalar_prefetch=3,
        grid=(t // tm,),
        in_specs=[smem_tok, smem_tok, pl.BlockSpec(memory_space=pl.ANY)],
        out_specs=pl.BlockSpec(memory_space=pl.ANY),
        scratch_shapes=[pltpu.VMEM((EXPERT_BLOCK, D_MODEL), F32),
                        pltpu.SemaphoreType.DMA(()), pltpu.SemaphoreType.DMA(())],
    )
    return pl.pallas_call(
        _scatter_body,
        grid_spec=grid_spec,
        out_shape=jax.ShapeDtypeStruct((rows, D_MODEL), F32),
        compiler_params=_cparams(("arbitrary",)),
        name="moe_scatter",
    )(pad_start, padded, n_used, idx, pos, x1)


def _expert_body(be_ref, x_ref, wg_ref, wu_ref, wd_ref, y_ref):
    xb = x_ref[...].astype(BF16)
    h = _silu(_dot(xb, wg_ref[0])) * _dot(xb, wu_ref[0])
    y_ref[...] = _dot(h.astype(BF16), wd_ref[0])


def _experts(block_e, xs, wg, wu, wd):
    rows = xs.shape[0]
    nb = rows // EXPERT_BLOCK
    grid_spec = pltpu.PrefetchScalarGridSpec(
        num_scalar_prefetch=1,
        grid=(nb,),
        in_specs=[pl.BlockSpec((EXPERT_BLOCK, D_MODEL), lambda b, be: (b, 0)),
                  pl.BlockSpec((1, D_MODEL, D_EXPERT), lambda b, be: (be[b], 0, 0)),
                  pl.BlockSpec((1, D_MODEL, D_EXPERT), lambda b, be: (be[b], 0, 0)),
                  pl.BlockSpec((1, D_EXPERT, D_MODEL), lambda b, be: (be[b], 0, 0))],
        out_specs=pl.BlockSpec((EXPERT_BLOCK, D_MODEL), lambda b, be: (b, 0)),
    )
    return pl.pallas_call(
        _expert_body,
        grid_spec=grid_spec,
        out_shape=jax.ShapeDtypeStruct((rows, D_MODEL), F32),
        compiler_params=_cparams(("parallel",)),
        name="moe_experts",
    )(block_e, xs, wg, wu, wd)


def _final_body(ps_ref, idx_ref, pos_ref, x1_ref, wt_ref, wsg_ref, wsu_ref, wsd_ref, lg_ref, lb_ref,
                ys_hbm, out_ref, gbuf_ref, sem):
    tm = x1_ref.shape[0]

    def issue(t, c):
        for k in range(TOP_K):
            d = ps_ref[idx_ref[k, t]] + pos_ref[k, t]
            pltpu.make_async_copy(ys_hbm.at[pl.ds(d, 1)], gbuf_ref.at[k, pl.ds(t, 1)], sem).start()
        return c

    lax.fori_loop(0, tm, issue, 0)
    x1 = x1_ref[...]
    xb = x1.astype(BF16)
    h = _silu(_dot(xb, wsg_ref[...])) * _dot(xb, wsu_ref[...])
    acc = DN_ALPHA * x1 + _dot(h.astype(BF16), wsd_ref[...])
    for k in range(TOP_K):
        pltpu.make_async_copy(ys_hbm.at[pl.ds(0, tm)], gbuf_ref.at[k], sem).wait()
    wt = wt_ref[...]
    for k in range(TOP_K):
        acc = acc + wt[:, k:k + 1] * gbuf_ref[k]
    out_ref[...] = _layer_norm_rows(acc, lg_ref[...], lb_ref[...])


def _final(pad_start, idx, pos, x1, wt_rows, wsg, wsu, wsd, lg, lb, ys, tm=128):
    t = x1.shape[0]
    smem_tok = pl.BlockSpec((TOP_K, tm), lambda i, ps: (0, i), memory_space=pltpu.SMEM)
    const = lambda w: pl.BlockSpec(w.shape, lambda i, ps: (0, 0))
    grid_spec = pltpu.PrefetchScalarGridSpec(
        num_scalar_prefetch=1,
        grid=(t // tm,),
        in_specs=[smem_tok, smem_tok, pl.BlockSpec((tm, D_MODEL), lambda i, ps: (i, 0)),
                  pl.BlockSpec((tm, TOP_K), lambda i, ps: (i, 0)),
                  const(wsg), const(wsu), const(wsd), const(lg), const(lb),
                  pl.BlockSpec(memory_space=pl.ANY)],
        out_specs=pl.BlockSpec((tm, D_MODEL), lambda i, ps: (i, 0)),
        scratch_shapes=[pltpu.VMEM((TOP_K, tm, D_MODEL), F32), pltpu.SemaphoreType.DMA(())],
    )
    return pl.pallas_call(
        _final_body,
        grid_spec=grid_spec,
        out_shape=jax.ShapeDtypeStruct((t, D_MODEL), F32),
        compiler_params=_cparams(("arbitrary",)),
        name="moe_combine_ln2",
    )(pad_start, idx, pos, x1, wt_rows, wsg, wsu, wsd, lg, lb, ys)


def _rot_cols(w):
    half = w.shape[-1] // 2
    return jnp.concatenate([-w[..., half:], w[..., :half]], axis=-1)


def _prepare(p, seq_lens):
    f = {}
    w_in = p["w_in"]
    o = np.cumsum((0, Q_LORA, KV_LORA, QK_ROPE, D_INNER, CONV_DIM, SSD_HEADS, SSD_HEADS, D_MODEL, D_MODEL))
    seg = lambda i: w_in[:, o[i]:o[i + 1]]
    zeros = lambda n: jnp.zeros((D_MODEL, n), F32)
    f["wq"] = jnp.concatenate([seg(0), seg(1), seg(2), _rot_cols(seg(2)), zeros(QKV_W - KROPE_OFF - 2 * QK_ROPE)],
                              axis=1).astype(BF16)
    f["wz"] = seg(3).astype(BF16)
    f["wx"] = seg(4).astype(BF16)
    f["wdt"] = jnp.concatenate([seg(5), seg(6), zeros(LANES - 2 * SSD_HEADS)], axis=1).astype(BF16)
    f["wg"] = jnp.concatenate([seg(7), seg(8)], axis=1).astype(BF16)

    f["qn"] = p["q_norm_w"].reshape(1, Q_LORA)
    f["kvn"] = p["kv_norm_w"].reshape(1, KV_LORA)
    w_uq = p["w_uq"]
    nope, rope = w_uq[..., :QK_NOPE], w_uq[..., QK_NOPE:]
    padq = HEAD_PAD - QK_NOPE - QK_ROPE
    zq = lambda n: jnp.zeros((Q_LORA, N_HEADS, n), F32)
    wq_a = jnp.concatenate([nope, rope, zq(padq)], axis=-1).reshape(Q_LORA, N_HEADS * HEAD_PAD)
    wq_b = jnp.concatenate([zq(QK_NOPE), _rot_cols(rope), zq(padq)], axis=-1).reshape(Q_LORA, N_HEADS * HEAD_PAD)
    f["w_q"] = jnp.concatenate([wq_a, wq_b], axis=1).astype(BF16)
    w_ukv = p["w_ukv"]
    wk_t = jnp.transpose(w_ukv[..., :QK_NOPE], (1, 2, 0))
    eye = jnp.eye(QK_ROPE, dtype=F32)
    rope_rows = jnp.concatenate([jnp.zeros((QK_ROPE, KV_LORA), F32), eye, eye,
                                 jnp.zeros((QK_ROPE, LANES - 2 * QK_ROPE), F32)], axis=1)
    wk = jnp.concatenate([
        jnp.concatenate([wk_t, jnp.zeros((N_HEADS, QK_NOPE, LANES), F32)], axis=2),
        jnp.broadcast_to(rope_rows, (N_HEADS, QK_ROPE, KV_LORA + LANES)),
        jnp.zeros((N_HEADS, padq, KV_LORA + LANES), F32)], axis=1)
    f["w_k"] = wk.reshape(N_HEADS * HEAD_PAD, KV_LORA + LANES).astype(BF16)
    f["w_v"] = w_ukv[..., QK_NOPE:].reshape(KV_LORA, N_HEADS * V_DIM).astype(BF16)

    scale = (QK_NOPE + QK_ROPE) ** -0.5
    f["rope"] = {}
    for s in sorted(set(seq_lens)):
        inv_freq = ROPE_THETA ** (-jnp.arange(0, QK_ROPE, 2, dtype=F32) / QK_ROPE)
        ang = jnp.arange(s, dtype=F32)[:, None] * inv_freq[None, :]
        cos, sin = jnp.cos(ang), jnp.sin(ang)
        one, zero = jnp.ones((s, QK_NOPE), F32), jnp.zeros((s, QK_NOPE), F32)
        zp = jnp.zeros((s, padq), F32)
        cq_t = scale * jnp.concatenate([one, cos, cos, zp], axis=1)
        sq_t = scale * jnp.concatenate([zero, sin, sin, zp], axis=1)
        tk_t = jnp.concatenate([cos, cos, sin, sin, jnp.zeros((s, LANES - 2 * QK_ROPE), F32)], axis=1)
        f["rope"][s] = (cq_t, sq_t, tk_t)

    f["conv_w"] = jnp.concatenate([p["conv_w"], jnp.zeros((8 - CONV_WIDTH, CONV_DIM), F32)], axis=0)
    f["conv_b"] = p["conv_b"].reshape(1, CONV_DIM)
    pad_h = jnp.zeros((LANES - 2 * SSD_HEADS,), F32)
    f["dt_bias"] = jnp.concatenate([p["dt_bias_f"], p["dt_bias_b"], pad_h]).reshape(1, LANES)
    f["a_log"] = jnp.concatenate([p["a_log_f"], p["a_log_b"], pad_h]).reshape(1, LANES)
    head_of_lane = jnp.arange(D_INNER, dtype=jnp.int32) // SSD_HEADDIM
    lane_id = jnp.arange(LANES, dtype=jnp.int32)[:, None]
    f["expand_f"] = (lane_id == head_of_lane[None, :]).astype(BF16)
    f["expand_b"] = (lane_id == head_of_lane[None, :] + SSD_HEADS).astype(BF16)
    f["d_skip"] = jnp.repeat(p["d_skip"], SSD_HEADDIM).reshape(1, D_INNER)
    f["ssd_nw"] = p["ssd_norm_w"].reshape(1, D_INNER)
    f["w_oa"] = p["w_o_attn"].astype(BF16)
    f["w_os"] = p["w_o_ssd"].astype(BF16)
    f["w_out"] = p["w_out"].astype(BF16)
    f["ln1_g"] = p["ln1_g"].reshape(1, D_MODEL)
    f["ln1_b"] = p["ln1_b"].reshape(1, D_MODEL)

    wr_t = p["w_router"].T
    wr_hi = wr_t.astype(BF16)
    f["wr_hi"] = wr_hi
    f["wr_lo"] = (wr_t - wr_hi.astype(F32)).astype(BF16)
    f["r_bias"] = p["router_bias"].reshape(N_EXPERTS, 1)
    f["w_gate"] = p["w_gate"].astype(BF16)
    f["w_up"] = p["w_up"].astype(BF16)
    f["w_down"] = p["w_down"].astype(BF16)
    f["ws_gate"] = p["ws_gate"].astype(BF16)
    f["ws_up"] = p["ws_up"].astype(BF16)
    f["ws_down"] = p["ws_down"].astype(BF16)
    f["ln2_g"] = p["ln2_g"].reshape(1, D_MODEL)
    f["ln2_b"] = p["ln2_b"].reshape(1, D_MODEL)
    return f


ROUTER_TILE = 512


def _layer(x, f):
    b, s, d = x.shape
    t = b * s
    x2d = x.reshape(t, d)
    qkv, z, xbc, dt, g = _in_proj(x2d, f["wq"], f["wz"], f["wx"], f["wdt"], f["wg"])

    cq_t, sq_t, tk_t = f["rope"][s]
    q, kt, v = _mla_prep(qkv.reshape(b, s, QKV_W), f["qn"], f["kvn"], f["w_q"], f["w_k"], f["w_v"],
                         cq_t, sq_t, tk_t)
    o = _attention(q, kt, v)

    conv_out = _conv(xbc.reshape(b, s, CONV_DIM), f["conv_w"], f["conv_b"])
    dt3 = dt.reshape(b, s, LANES)
    y_f = _ssd(conv_out, dt3, f["dt_bias"], f["a_log"], f["expand_f"], reverse=False)
    y_b = _ssd(conv_out, dt3, f["dt_bias"], f["a_log"], f["expand_b"], reverse=True)

    x1 = _mix(x2d, o.reshape(t, -1), y_f.reshape(t, -1), y_b.reshape(t, -1), conv_out.reshape(t, -1), z, g,
              f["w_oa"], f["w_os"], f["w_out"], f["d_skip"], f["ssd_nw"], f["ln1_g"], f["ln1_b"])

    su = (jnp.arange(ROUTER_TILE)[:, None] < jnp.arange(ROUTER_TILE)[None, :]).astype(BF16)
    idx, wts, pos, cnt = _router(x1, f["wr_hi"], f["wr_lo"], f["r_bias"], su, tm=ROUTER_TILE)

    counts = cnt[:, 0].astype(jnp.int32)
    padded = (counts + EXPERT_BLOCK - 1) // EXPERT_BLOCK * EXPERT_BLOCK
    pad_end = jnp.cumsum(padded)
    pad_start = pad_end - padded
    nb = t * TOP_K // EXPERT_BLOCK + N_EXPERTS
    n_used = pad_end[-1] // EXPERT_BLOCK
    blk_row = jnp.minimum(jnp.arange(nb, dtype=jnp.int32), n_used - 1) * EXPERT_BLOCK
    block_e = jnp.minimum(jnp.searchsorted(pad_end, blk_row, side="right"), N_EXPERTS - 1).astype(jnp.int32)

    xs = _scatter(pad_start, padded, n_used.reshape(1), idx, pos, x1, nb * EXPERT_BLOCK)
    ys = _experts(block_e, xs, f["w_gate"], f["w_up"], f["w_down"])
    y = _final(pad_start, idx, pos, x1, wts.T, f["ws_gate"], f["ws_up"], f["ws_down"],
               f["ln2_g"], f["ln2_b"], ys)
    return y.reshape(b, s, d)


def kernel(x_prompt, x_sample, w_in, q_norm_w, kv_norm_w, w_uq, w_ukv, w_o_attn, conv_w, conv_b, dt_bias_f,
           dt_bias_b, a_log_f, a_log_b, d_skip, ssd_norm_w, w_o_ssd, w_out, ln1_g, ln1_b, w_router, router_bias,
           w_gate, w_up, w_down, ws_gate, ws_up, ws_down, ln2_g, ln2_b):
    params = dict(w_in=w_in, q_norm_w=q_norm_w, kv_norm_w=kv_norm_w, w_uq=w_uq, w_ukv=w_ukv, w_o_attn=w_o_attn,
                  conv_w=conv_w, conv_b=conv_b, dt_bias_f=dt_bias_f, dt_bias_b=dt_bias_b, a_log_f=a_log_f,
                  a_log_b=a_log_b, d_skip=d_skip, ssd_norm_w=ssd_norm_w, w_o_ssd=w_o_ssd, w_out=w_out,
                  ln1_g=ln1_g, ln1_b=ln1_b, w_router=w_router, router_bias=router_bias, w_gate=w_gate,
                  w_up=w_up, w_down=w_down, ws_gate=ws_gate, ws_up=ws_up, ws_down=ws_down, ln2_g=ln2_g,
                  ln2_b=ln2_b)
    assert w_in.shape[0] == DEPTH == 1
    params = {k: v[0] for k, v in params.items()}
    f = _prepare(params, (x_prompt.shape[1], x_sample.shape[1]))
    return (_layer(x_prompt, f), _layer(x_sample, f))
```

```python
import functools
import math

import jax
import jax.numpy as jnp
import numpy as np
from jax import lax
from jax.experimental import pallas as pl
from jax.experimental.pallas import tpu as pltpu

F32 = jnp.float32
BF16 = jnp.bfloat16

D_MODEL = 1024
DEPTH = 1
N_HEADS = 16
QK_NOPE = 64
QK_ROPE = 32
V_DIM = 64
Q_LORA = 384
KV_LORA = 256
ROPE_THETA = 10000.0
D_INNER = 2 * D_MODEL
SSD_HEADDIM = 64
SSD_HEADS = D_INNER // SSD_HEADDIM
SSD_GROUPS = 4
D_STATE = 128
CONV_WIDTH = 5
CONV_DIM = D_INNER + 2 * SSD_GROUPS * D_STATE
CHUNK = 128
N_EXPERTS = 64
TOP_K = 8
N_EXPERT_GROUPS = 8
TOPK_GROUPS = 4
D_EXPERT = D_MODEL // 4
D_SHARED = D_MODEL // 4
ROUTED_SCALE = 2.5
LN_EPS = 1e-5
RMS_EPS = 1e-6
DN_ALPHA = (2 * DEPTH) ** 0.25

LANES = 128
HEAD_PAD = 128
QKV_W = 768
KROPE_OFF = Q_LORA + KV_LORA
EXPERT_BLOCK = 256
VMEM_LIMIT = 56 * 1024 * 1024


def _cparams(sem, vmem=VMEM_LIMIT):
    return pltpu.CompilerParams(dimension_semantics=sem, vmem_limit_bytes=vmem)


def _dot(a, b):
    return jnp.dot(a, b, preferred_element_type=F32)


def _dot_nt(a, b):
    return lax.dot_general(a, b, (((1,), (1,)), ((), ())), preferred_element_type=F32)


def _split3(a):
    h = a.astype(BF16)
    r = a - h.astype(F32)
    m = r.astype(BF16)
    l = (r - m.astype(F32)).astype(BF16)
    return h, m, l


def _dot_split_lhs(a_f32, b_bf16):
    h, m, l = _split3(a_f32)
    return _dot(h, b_bf16) + _dot(m, b_bf16) + _dot(l, b_bf16)


def _silu(x):
    return x * jax.nn.sigmoid(x)


def _layer_norm_rows(r, g, b):
    mu = jnp.mean(r, axis=-1, keepdims=True)
    d = r - mu
    var = jnp.mean(d * d, axis=-1, keepdims=True)
    return d * lax.rsqrt(var + LN_EPS) * g + b


def _inproj_body(x_ref, wq_ref, wz_ref, wx_ref, wdt_ref, wg_ref,
                 qkv_ref, z_ref, xbc_ref, dt_ref, g_ref):
    xb = x_ref[...].astype(BF16)
    qkv_ref[...] = _dot(xb, wq_ref[...]).astype(BF16)
    z_ref[...] = _silu(_dot(xb, wz_ref[...])).astype(BF16)
    xbc_ref[...] = _dot(xb, wx_ref[...]).astype(BF16)
    dt_ref[...] = _dot(xb, wdt_ref[...])
    g_ref[...] = jax.nn.sigmoid(_dot(xb, wg_ref[...])).astype(BF16)


def _in_proj(x2d, wq, wz, wx, wdt, wg, tm=512):
    t = x2d.shape[0]
    const = lambda w: pl.BlockSpec(w.shape, lambda i: (0, 0), pipeline_mode=pl.Buffered(1))
    row = lambda n: pl.BlockSpec((tm, n), lambda i: (i, 0))
    return pl.pallas_call(
        _inproj_body,
        grid=(t // tm,),
        in_specs=[row(D_MODEL), const(wq), const(wz), const(wx), const(wdt), const(wg)],
        out_specs=[row(QKV_W), row(D_INNER), row(CONV_DIM), row(LANES), row(2 * D_MODEL)],
        out_shape=[jax.ShapeDtypeStruct((t, QKV_W), BF16),
                   jax.ShapeDtypeStruct((t, D_INNER), BF16),
                   jax.ShapeDtypeStruct((t, CONV_DIM), BF16),
                   jax.ShapeDtypeStruct((t, LANES), F32),
                   jax.ShapeDtypeStruct((t, 2 * D_MODEL), BF16)],
        compiler_params=_cparams(("parallel",)),
        name="in_proj",
    )(x2d, wq, wz, wx, wdt, wg)


def _mla_prep_body(qkv_ref, qn_ref, kvn_ref, wq_ref, wk_ref, wv_ref, vone_ref, cq_ref, sq_ref, tk_ref,
                   q_ref, kt_ref, v_ref):
    qkv = qkv_ref[0].astype(F32)
    cq = qkv[:, :Q_LORA]
    cq = cq * lax.rsqrt(jnp.mean(cq * cq, axis=-1, keepdims=True) + RMS_EPS) * qn_ref[...]
    ckv = qkv[:, Q_LORA:KROPE_OFF]
    ckv = ckv * lax.rsqrt(jnp.mean(ckv * ckv, axis=-1, keepdims=True) + RMS_EPS) * kvn_ref[...]
    ckv_b = ckv.astype(BF16)
    qq = _dot(cq.astype(BF16), wq_ref[...])
    half = N_HEADS * HEAD_PAD
    cq_t = cq_ref[...]
    sq_t = sq_ref[...]
    for h in range(N_HEADS):
        lo = h * HEAD_PAD
        q_ref[0, :, lo:lo + HEAD_PAD] = (qq[:, lo:lo + HEAD_PAD] * cq_t
                                          + qq[:, half + lo:half + lo + HEAD_PAD] * sq_t).astype(BF16)
    kr = (qkv[:, KROPE_OFF:] * tk_ref[...]).astype(BF16)
    e = jnp.concatenate([ckv_b, kr], axis=1)
    kt_ref[0] = _dot_nt(wk_ref[...], e).astype(BF16)
    v_ref[0] = (_dot(ckv_b, wv_ref[...]) + vone_ref[...]).astype(BF16)


def _mla_prep(qkv, qn, kvn, wq, wk, wv, vone, cq_t, sq_t, tk_t, ts=512):
    b, s, _ = qkv.shape
    const = lambda w: pl.BlockSpec(w.shape, lambda bi, i: (0, 0))
    tab = pl.BlockSpec((ts, LANES), lambda bi, i: (i, 0))
    hw = N_HEADS * HEAD_PAD
    return pl.pallas_call(
        _mla_prep_body,
        grid=(b, s // ts),
        in_specs=[pl.BlockSpec((1, ts, QKV_W), lambda bi, i: (bi, i, 0)),
                  const(qn), const(kvn), const(wq), const(wk), const(wv), const(vone), tab, tab, tab],
        out_specs=[pl.BlockSpec((1, ts, hw), lambda bi, i: (bi, i, 0)),
                   pl.BlockSpec((1, hw, ts), lambda bi, i: (bi, 0, i)),
                   pl.BlockSpec((1, ts, hw), lambda bi, i: (bi, i, 0))],
        out_shape=[jax.ShapeDtypeStruct((b, s, hw), BF16),
                   jax.ShapeDtypeStruct((b, hw, s), BF16),
                   jax.ShapeDtypeStruct((b, s, hw), BF16)],
        compiler_params=_cparams(("parallel", "parallel")),
        name="mla_prep",
    )(qkv, qn, kvn, wq, wk, wv, vone, cq_t, sq_t, tk_t)


def _attn_body(q_ref, kt_ref, v_ref, o_ref):
    tq = q_ref.shape[1]
    for r in range(tq // ATTN_ROWS):
        rows = slice(r * ATTN_ROWS, (r + 1) * ATTN_ROWS)
        outs = []
        for hh in range(2):
            sl = slice(hh * HEAD_PAD, (hh + 1) * HEAD_PAD)
            s = _dot(q_ref[0, rows, sl], kt_ref[0, sl, :])
            m = jnp.max(s, axis=-1, keepdims=True)
            p = jnp.exp2(s - m).astype(BF16)
            o = _dot(p, v_ref[0, :, sl])
            outs.append(o / o[:, V_DIM:V_DIM + 1])
        lane = lax.broadcasted_iota(jnp.int32, outs[0].shape, 1)
        o_ref[0, rows, :] = jnp.where(lane < V_DIM, outs[0], pltpu.roll(outs[1], V_DIM, axis=1)).astype(BF16)


ATTN_ROWS = 256


def _attention(q, kt, v, tq=1024):
    b, s, _ = q.shape
    pairs = N_HEADS // 2
    return pl.pallas_call(
        _attn_body,
        grid=(b, pairs, s // tq),
        in_specs=[pl.BlockSpec((1, tq, 2 * HEAD_PAD), lambda bi, j, i: (bi, i, j)),
                  pl.BlockSpec((1, 2 * HEAD_PAD, s), lambda bi, j, i: (bi, j, 0)),
                  pl.BlockSpec((1, s, 2 * HEAD_PAD), lambda bi, j, i: (bi, 0, j))],
        out_specs=pl.BlockSpec((1, tq, 2 * V_DIM), lambda bi, j, i: (bi, i, j)),
        out_shape=jax.ShapeDtypeStruct((b, s, N_HEADS * V_DIM), BF16),
        compiler_params=_cparams(("parallel", "parallel", "arbitrary")),
        name="attention",
    )(q, kt, v)


CONV_HALO = 16


def _conv_body(prev_ref, main_ref, next_ref, w_ref, b_ref, o_ref, ext_ref):
    i = pl.program_id(1)
    n = pl.num_programs(1)
    ts = main_ref.shape[1]
    prev = prev_ref[0].astype(F32)
    nxt = next_ref[0].astype(F32)
    ext_ref[0:CONV_HALO, :] = jnp.where(i > 0, prev, 0.0)
    ext_ref[CONV_HALO:CONV_HALO + ts, :] = main_ref[0].astype(F32)
    ext_ref[CONV_HALO + ts:, :] = jnp.where(i < n - 1, nxt, 0.0)
    acc = b_ref[...]
    for k in range(CONV_WIDTH):
        off = CONV_HALO - CONV_WIDTH // 2 + k
        acc = acc + ext_ref[off:off + ts, :] * w_ref[k:k + 1, :]
    o_ref[0] = _silu(acc).astype(BF16)


def _conv(xbc, w8, bias, ts=512, tc=512):
    b, s, c = xbc.shape
    r = ts // CONV_HALO
    last = s // CONV_HALO - 1
    return pl.pallas_call(
        _conv_body,
        grid=(b, s // ts, c // tc),
        in_specs=[pl.BlockSpec((1, CONV_HALO, tc), lambda bi, i, j: (bi, jnp.maximum(i * r - 1, 0), j)),
                  pl.BlockSpec((1, ts, tc), lambda bi, i, j: (bi, i, j)),
                  pl.BlockSpec((1, CONV_HALO, tc), lambda bi, i, j: (bi, jnp.minimum((i + 1) * r, last), j)),
                  pl.BlockSpec((8, tc), lambda bi, i, j: (0, j)),
                  pl.BlockSpec((1, tc), lambda bi, i, j: (0, j))],
        out_specs=pl.BlockSpec((1, ts, tc), lambda bi, i, j: (bi, i, j)),
        out_shape=jax.ShapeDtypeStruct((b, s, c), BF16),
        scratch_shapes=[pltpu.VMEM((ts + 2 * CONV_HALO, tc), F32)],
        compiler_params=_cparams(("parallel", "parallel", "parallel")),
        name="conv",
    )(xbc, xbc, xbc, w8, bias)


def _ssd_body(reverse, xs_ref, b_ref, c_ref, dt_ref, dtb_ref, alog_ref, e_ref, y_ref, state_ref):
    @pl.when(pl.program_id(1) == 0)
    def _():
        state_ref[...] = jnp.zeros_like(state_ref)

    q = CHUNK
    lane0 = SSD_HEADS if reverse else 0
    rows = lax.broadcasted_iota(jnp.int32, (q, q), 0)
    cols = lax.broadcasted_iota(jnp.int32, (q, q), 1)
    mask = (rows <= cols) if reverse else (rows >= cols)
    tri = mask.astype(BF16)

    x = dt_ref[0] + dtb_ref[...]
    dt = jnp.maximum(x, 0.0) + jnp.log(1.0 + jnp.exp(-jnp.abs(x)))
    da = dt * (-jnp.exp(alog_ref[...]))
    cum = _dot_split_lhs_rhs(tri, da)
    cum_t = cum.T
    dt_t = dt.T
    total = cum[0:1, :] if reverse else cum[q - 1:q, :]
    w = dt * jnp.exp(total - cum)
    w_exp = _dot(w.astype(BF16), e_ref[...])
    xs = xs_ref[0]
    x_dec = (xs.astype(F32) * w_exp).astype(BF16)

    bm = b_ref[0].astype(F32)
    cm = c_ref[0]
    cm_f = cm.astype(F32)
    state_old = state_ref[...].astype(BF16)
    gw = D_STATE
    hp = SSD_HEADS // SSD_GROUPS * SSD_HEADDIM
    new_states = []
    for g in range(SSD_GROUPS):
        bt = bm[:, g * gw:(g + 1) * gw].T.astype(BF16)
        new_states.append(_dot(bt, x_dec[:, g * hp:(g + 1) * hp]))
        cb = _dot(cm[:, g * gw:(g + 1) * gw], bt)
        c_g = cm_f[:, g * gw:(g + 1) * gw]
        for jj in range(hp // LANES):
            j = g * (hp // LANES) + jj
            sl = slice(j * LANES, (j + 1) * LANES)
            rhs = jnp.concatenate([xs[:, sl], state_old[:, sl]], axis=0)
            ys = []
            for hh in range(2):
                ln = lane0 + 2 * j + hh
                col = jnp.broadcast_to(cum[:, ln:ln + 1], (q, q))
                seg = col - cum_t[ln:ln + 1, :]
                decay = jnp.exp(jnp.where(mask, seg, -jnp.inf))
                m = cb * decay * dt_t[ln:ln + 1, :]
                cs = c_g * jnp.exp(col)
                lhs = jnp.concatenate([m.astype(BF16), cs.astype(BF16)], axis=1)
                ys.append(_dot(lhs, rhs))
            lane = lax.broadcasted_iota(jnp.int32, (q, LANES), 1)
            y_ref[0, :, sl] = jnp.where(lane < SSD_HEADDIM, ys[0], ys[1]).astype(BF16)

    dec = jnp.broadcast_to(jnp.exp(total), (8, LANES))
    dec_exp = _dot_split_lhs(dec, e_ref[...])[0:1, :]
    state_ref[...] = state_ref[...] * dec_exp + jnp.concatenate(new_states, axis=1)


def _dot_split_lhs_rhs(tri_bf16, da_f32):
    h, m, l = _split3(da_f32)
    return _dot(tri_bf16, h) + _dot(tri_bf16, m) + _dot(tri_bf16, l)


def _ssd(conv_out, dt, dt_bias, a_log, expand, reverse):
    b, s, _ = conv_out.shape
    nc = s // CHUNK
    cidx = (lambda c: nc - 1 - c) if reverse else (lambda c: c)
    bcol = D_INNER // (SSD_GROUPS * D_STATE)
    gn = SSD_GROUPS * D_STATE
    const = lambda w: pl.BlockSpec(w.shape, lambda bi, c: (0, 0))
    return pl.pallas_call(
        functools.partial(_ssd_body, reverse),
        grid=(b, nc),
        in_specs=[pl.BlockSpec((1, CHUNK, D_INNER), lambda bi, c: (bi, cidx(c), 0)),
                  pl.BlockSpec((1, CHUNK, gn), lambda bi, c: (bi, cidx(c), bcol)),
                  pl.BlockSpec((1, CHUNK, gn), lambda bi, c: (bi, cidx(c), bcol + 1)),
                  pl.BlockSpec((1, CHUNK, LANES), lambda bi, c: (bi, cidx(c), 0)),
                  const(dt_bias), const(a_log), const(expand)],
        out_specs=pl.BlockSpec((1, CHUNK, D_INNER), lambda bi, c: (bi, cidx(c), 0)),
        out_shape=jax.ShapeDtypeStruct((b, s, D_INNER), BF16),
        scratch_shapes=[pltpu.VMEM((D_STATE, D_INNER), F32)],
        compiler_params=_cparams(("parallel", "arbitrary")),
        name="ssd_bwd" if reverse else "ssd_fwd",
    )(conv_out, conv_out, conv_out, dt, dt_bias, a_log, expand)


def _mix_body(x_ref, o_ref, yf_ref, yb_ref, xs_ref, z_ref, g_ref,
              woa_ref, wos_ref, wout_ref, dsk_ref, nw_ref, lg_ref, lb_ref, x1_ref):
    attn = _dot(o_ref[...], woa_ref[...])
    y = yf_ref[...].astype(F32) + yb_ref[...].astype(F32) + dsk_ref[...] * xs_ref[...].astype(F32)
    y = y * z_ref[...].astype(F32)
    yn = y * lax.rsqrt(jnp.mean(y * y, axis=-1, keepdims=True) + RMS_EPS) * nw_ref[...]
    ssd = _dot(yn.astype(BF16), wos_ref[...])
    g = g_ref[...].astype(F32)
    mixed_in = g[:, :D_MODEL] * attn + g[:, D_MODEL:] * ssd
    mixed = _dot(mixed_in.astype(BF16), wout_ref[...])
    x1_ref[...] = _layer_norm_rows(DN_ALPHA * x_ref[...] + mixed, lg_ref[...], lb_ref[...])


def _mix(x2d, o, yf, yb, conv_out, z, g, woa, wos, wout, dsk, nw, lg, lb, tm=256):
    t = x2d.shape[0]
    const = lambda w: pl.BlockSpec(w.shape, lambda i: (0, 0))
    row = lambda n: pl.BlockSpec((tm, n), lambda i: (i, 0))
    return pl.pallas_call(
        _mix_body,
        grid=(t // tm,),
        in_specs=[row(D_MODEL), row(D_MODEL), row(D_INNER), row(D_INNER), row(D_INNER), row(D_INNER),
                  row(2 * D_MODEL), const(woa), const(wos), const(wout), const(dsk), const(nw),
                  const(lg), const(lb)],
        out_specs=row(D_MODEL),
        out_shape=jax.ShapeDtypeStruct((t, D_MODEL), F32),
        compiler_params=_cparams(("parallel",)),
        name="mix_ln1",
    )(x2d, o, yf, yb, conv_out, z, g, woa, wos, wout, dsk, nw, lg, lb)


def _router_body(x_ref, wh_ref, wl_ref, bias_ref, su_ref, idx_ref, wts_ref, pos_ref, cnt_ref, carry_ref):
    @pl.when(pl.program_id(0) == 0)
    def _():
        carry_ref[...] = jnp.zeros_like(carry_ref)

    tm = x_ref.shape[0]
    x = x_ref[...]
    xh = x.astype(BF16)
    xl = (x - xh.astype(F32)).astype(BF16)
    wh = wh_ref[...]
    logits = _dot_nt(wh, xh) + _dot_nt(wh, xl) + _dot_nt(wl_ref[...], xh)
    scores = jax.nn.sigmoid(logits)
    choice = scores + bias_ref[...]
    gsz = N_EXPERTS // N_EXPERT_GROUPS
    shp = (N_EXPERT_GROUPS, gsz, tm)
    ch = choice.reshape(shp)
    sc = scores.reshape(shp)
    jio = lax.broadcasted_iota(jnp.int32, shp, 1)
    gio = lax.broadcasted_iota(jnp.int32, shp, 0)
    eio = gio * gsz + jio
    neg = -jnp.inf

    m1 = jnp.max(ch, axis=1, keepdims=True)
    i1 = jnp.min(jnp.where(ch == m1, jio, gsz), axis=1, keepdims=True)
    m2 = jnp.max(jnp.where(jio == i1, neg, ch), axis=1, keepdims=True)
    gs = m1 + m2
    gio1 = lax.broadcasted_iota(jnp.int32, (N_EXPERT_GROUPS, 1, tm), 0)
    sel = jnp.zeros(gs.shape, jnp.bool_)
    cur = gs
    for _ in range(TOPK_GROUPS):
        m = jnp.max(cur, axis=0, keepdims=True)
        gi = jnp.min(jnp.where(cur == m, gio1, N_EXPERT_GROUPS), axis=0, keepdims=True)
        hit = gio1 == gi
        sel = jnp.logical_or(sel, hit)
        cur = jnp.where(hit, neg, cur)
    masked = jnp.where(sel, ch, neg)

    def red(op, v):
        return op(op(v, axis=1, keepdims=True), axis=0, keepdims=True)

    hits, idxs, ws = [], [], []
    for _ in range(TOP_K):
        m = red(jnp.max, masked)
        ei = red(jnp.min, jnp.where(masked == m, eio, N_EXPERTS))
        hit = eio == ei
        hits.append(hit)
        idxs.append(ei)
        ws.append(red(jnp.sum, jnp.where(hit, sc, 0.0)))
        masked = jnp.where(hit, neg, masked)
    wsum = ws[0]
    for k in range(1, TOP_K):
        wsum = wsum + ws[k]

    onehot = hits[0]
    for k in range(1, TOP_K):
        onehot = jnp.logical_or(onehot, hits[k])
    oh = onehot.astype(F32).reshape(N_EXPERTS, tm)
    before = _dot(oh.astype(BF16), su_ref[...]) + carry_ref[...]
    before3 = before.reshape(shp)
    for k in range(TOP_K):
        idx_ref[k:k + 1, :] = idxs[k].reshape(1, tm)
        wts_ref[k:k + 1, :] = (ws[k] / wsum * ROUTED_SCALE).reshape(1, tm)
        pos_ref[k:k + 1, :] = red(jnp.sum, jnp.where(hits[k], before3, 0.0)).reshape(1, tm).astype(jnp.int32)
    carry_ref[...] = carry_ref[...] + jnp.sum(oh, axis=1, keepdims=True)
    cnt_ref[...] = jnp.broadcast_to(carry_ref[...], cnt_ref.shape)


def _router(x1, wr_hi, wr_lo, bias, su, tm=512):
    t = x1.shape[0]
    const = lambda w: pl.BlockSpec(w.shape, lambda i: (0, 0))
    out_tok = pl.BlockSpec((TOP_K, tm), lambda i: (0, i))
    return pl.pallas_call(
        _router_body,
        grid=(t // tm,),
        in_specs=[pl.BlockSpec((tm, D_MODEL), lambda i: (i, 0)), const(wr_hi), const(wr_lo), const(bias),
                  const(su)],
        out_specs=[out_tok, out_tok, out_tok, pl.BlockSpec((N_EXPERTS, LANES), lambda i: (0, 0))],
        out_shape=[jax.ShapeDtypeStruct((TOP_K, t), jnp.int32),
                   jax.ShapeDtypeStruct((TOP_K, t), F32),
                   jax.ShapeDtypeStruct((TOP_K, t), jnp.int32),
                   jax.ShapeDtypeStruct((N_EXPERTS, LANES), F32)],
        scratch_shapes=[pltpu.VMEM((N_EXPERTS, 1), F32)],
        compiler_params=_cparams(("arbitrary",)),
        name="router",
    )(x1, wr_hi, wr_lo, bias, su)


def _row_copy_wait(src_ref, dst_hbm, sem, n_rows, times):
    for _ in range(times):
        pltpu.make_async_copy(src_ref.at[pl.ds(0, n_rows)], dst_hbm.at[pl.ds(0, n_rows)], sem).wait()


def _scatter_body(ps_ref, pd_ref, nu_ref, dest_ref, x_ref, out_hbm, zero_ref, sem, zsem):
    tm = x_ref.shape[0]
    nb = out_hbm.shape[0] // EXPERT_BLOCK

    @pl.when(pl.program_id(0) == 0)
    def _():
        zero_ref[...] = jnp.zeros_like(zero_ref)

        def zcopy(row):
            start = pl.multiple_of(row, EXPERT_BLOCK)
            return pltpu.make_async_copy(zero_ref, out_hbm.at[pl.ds(start, EXPERT_BLOCK)], zsem)

        def last_block(e):
            return ps_ref[e] + pd_ref[e] - EXPERT_BLOCK

        def zstart(e, c):
            @pl.when(pd_ref[e] > 0)
            def _():
                zcopy(last_block(e)).start()
            return c

        def zwait(e, c):
            @pl.when(pd_ref[e] > 0)
            def _():
                zcopy(last_block(e)).wait()
            return c

        def tstart(b, c):
            zcopy(b * EXPERT_BLOCK).start()
            return c

        def twait(b, c):
            zcopy(b * EXPERT_BLOCK).wait()
            return c

        lax.fori_loop(0, N_EXPERTS, zstart, 0)
        lax.fori_loop(nu_ref[0], nb, tstart, 0)
        lax.fori_loop(0, N_EXPERTS, zwait, 0)
        lax.fori_loop(nu_ref[0], nb, twait, 0)

    def issue(t, c):
        for k in range(TOP_K):
            d = dest_ref[t * TOP_K + k]
            pltpu.make_async_copy(x_ref.at[pl.ds(t, 1)], out_hbm.at[pl.ds(d, 1)], sem).start()
        return c

    lax.fori_loop(0, tm, issue, 0)
    _row_copy_wait(x_ref, out_hbm, sem, tm, TOP_K)


def _scatter(pad_start, padded, n_used, dest_flat, x1, rows, tm=256):
    t = x1.shape[0]
    smem_tok = pl.BlockSpec((tm * TOP_K,), lambda i, ps, pd, nu: (i,), memory_space=pltpu.SMEM)
    grid_spec = pltpu.PrefetchScalarGridSpec(
        num_scalar_prefetch=3,
        grid=(t // tm,),
        in_specs=[smem_tok, pl.BlockSpec((tm, D_MODEL), lambda i, ps, pd, nu: (i, 0))],
        out_specs=pl.BlockSpec(memory_space=pl.ANY),
        scratch_shapes=[pltpu.VMEM((EXPERT_BLOCK, D_MODEL), F32),
                        pltpu.SemaphoreType.DMA(()), pltpu.SemaphoreType.DMA(())],
    )
    return pl.pallas_call(
        _scatter_body,
        grid_spec=grid_spec,
        out_shape=jax.ShapeDtypeStruct((rows, D_MODEL), F32),
        compiler_params=_cparams(("arbitrary",)),
        name="moe_scatter",
    )(pad_start, padded, n_used, dest_flat, x1)


def _expert_body(be_ref, x_ref, wg_ref, wu_ref, wd_ref, y_ref):
    xb = x_ref[...].astype(BF16)
    h = _silu(_dot(xb, wg_ref[0])) * _dot(xb, wu_ref[0])
    y_ref[...] = _dot(h.astype(BF16), wd_ref[0])


def _experts(block_e, xs, wg, wu, wd):
    rows = xs.shape[0]
    nb = rows // EXPERT_BLOCK
    grid_spec = pltpu.PrefetchScalarGridSpec(
        num_scalar_prefetch=1,
        grid=(nb,),
        in_specs=[pl.BlockSpec((EXPERT_BLOCK, D_MODEL), lambda b, be: (b, 0)),
                  pl.BlockSpec((1, D_MODEL, D_EXPERT), lambda b, be: (be[b], 0, 0)),
                  pl.BlockSpec((1, D_MODEL, D_EXPERT), lambda b, be: (be[b], 0, 0)),
                  pl.BlockSpec((1, D_EXPERT, D_MODEL), lambda b, be: (be[b], 0, 0))],
        out_specs=pl.BlockSpec((EXPERT_BLOCK, D_MODEL), lambda b, be: (b, 0)),
    )
    return pl.pallas_call(
        _expert_body,
        grid_spec=grid_spec,
        out_shape=jax.ShapeDtypeStruct((rows, D_MODEL), F32),
        compiler_params=_cparams(("parallel",)),
        name="moe_experts",
    )(block_e, xs, wg, wu, wd)


def _final_body(dest_ref, x1_ref, wt_ref, wsg_ref, wsu_ref, wsd_ref, lg_ref, lb_ref,
                ys_hbm, out_ref, gbuf_ref, sem):
    tm = x1_ref.shape[0]

    def issue(t, c):
        for k in range(TOP_K):
            d = dest_ref[t * TOP_K + k]
            pltpu.make_async_copy(ys_hbm.at[pl.ds(d, 1)], gbuf_ref.at[k, pl.ds(t, 1)], sem).start()
        return c

    lax.fori_loop(0, tm, issue, 0)
    x1 = x1_ref[...]
    xb = x1.astype(BF16)
    h = _silu(_dot(xb, wsg_ref[...])) * _dot(xb, wsu_ref[...])
    acc = DN_ALPHA * x1 + _dot(h.astype(BF16), wsd_ref[...])
    for k in range(TOP_K):
        pltpu.make_async_copy(ys_hbm.at[pl.ds(0, tm)], gbuf_ref.at[k], sem).wait()
    wt = wt_ref[...]
    for k in range(TOP_K):
        acc = acc + wt[:, k:k + 1] * gbuf_ref[k]
    out_ref[...] = _layer_norm_rows(acc, lg_ref[...], lb_ref[...])


def _final(dest_flat, x1, wt_rows, wsg, wsu, wsd, lg, lb, ys, tm=128):
    t = x1.shape[0]
    const = lambda w: pl.BlockSpec(w.shape, lambda i: (0, 0))
    return pl.pallas_call(
        _final_body,
        grid=(t // tm,),
        in_specs=[pl.BlockSpec((tm * TOP_K,), lambda i: (i,), memory_space=pltpu.SMEM),
                  pl.BlockSpec((tm, D_MODEL), lambda i: (i, 0)),
                  pl.BlockSpec((tm, TOP_K), lambda i: (i, 0)),
                  const(wsg), const(wsu), const(wsd), const(lg), const(lb),
                  pl.BlockSpec(memory_space=pl.ANY)],
        out_specs=pl.BlockSpec((tm, D_MODEL), lambda i: (i, 0)),
        out_shape=jax.ShapeDtypeStruct((t, D_MODEL), F32),
        scratch_shapes=[pltpu.VMEM((TOP_K, tm, D_MODEL), F32), pltpu.SemaphoreType.DMA(())],
        compiler_params=_cparams(("arbitrary",)),
        name="moe_combine_ln2",
    )(dest_flat, x1, wt_rows, wsg, wsu, wsd, lg, lb, ys)


def _dest_body(ps_ref, idx_ref, pos_ref, out_ref):
    idx = idx_ref[...]
    acc = pos_ref[...]
    for e in range(N_EXPERTS):
        acc = acc + jnp.where(idx == e, ps_ref[e], 0)
    out_ref[...] = acc


def _dest_rows(pad_start, idx, pos, tn=2048):
    t = idx.shape[1]
    tn = min(tn, t)
    tok = pl.BlockSpec((TOP_K, tn), lambda i, ps: (0, i))
    grid_spec = pltpu.PrefetchScalarGridSpec(
        num_scalar_prefetch=1, grid=(t // tn,), in_specs=[tok, tok], out_specs=tok)
    return pl.pallas_call(
        _dest_body,
        grid_spec=grid_spec,
        out_shape=jax.ShapeDtypeStruct((TOP_K, t), jnp.int32),
        compiler_params=_cparams(("parallel",)),
        name="moe_dest",
    )(pad_start, idx, pos)


def _rot_cols(w):
    half = w.shape[-1] // 2
    return jnp.concatenate([-w[..., half:], w[..., :half]], axis=-1)


def _prepare(p, seq_lens):
    f = {}
    w_in = p["w_in"]
    o = np.cumsum((0, Q_LORA, KV_LORA, QK_ROPE, D_INNER, CONV_DIM, SSD_HEADS, SSD_HEADS, D_MODEL, D_MODEL))
    seg = lambda i: w_in[:, o[i]:o[i + 1]]
    zeros = lambda n: jnp.zeros((D_MODEL, n), F32)
    f["wq"] = jnp.concatenate([seg(0), seg(1), seg(2), _rot_cols(seg(2)), zeros(QKV_W - KROPE_OFF - 2 * QK_ROPE)],
                              axis=1).astype(BF16)
    f["wz"] = seg(3).astype(BF16)
    f["wx"] = seg(4).astype(BF16)
    f["wdt"] = jnp.concatenate([seg(5), seg(6), zeros(LANES - 2 * SSD_HEADS)], axis=1).astype(BF16)
    f["wg"] = jnp.concatenate([seg(7), seg(8)], axis=1).astype(BF16)

    f["qn"] = p["q_norm_w"].reshape(1, Q_LORA)
    f["kvn"] = p["kv_norm_w"].reshape(1, KV_LORA)
    w_uq = p["w_uq"]
    nope, rope = w_uq[..., :QK_NOPE], w_uq[..., QK_NOPE:]
    padq = HEAD_PAD - QK_NOPE - QK_ROPE
    zq = lambda n: jnp.zeros((Q_LORA, N_HEADS, n), F32)
    wq_a = jnp.concatenate([nope, rope, zq(padq)], axis=-1).reshape(Q_LORA, N_HEADS * HEAD_PAD)
    wq_b = jnp.concatenate([zq(QK_NOPE), _rot_cols(rope), zq(padq)], axis=-1).reshape(Q_LORA, N_HEADS * HEAD_PAD)
    f["w_q"] = jnp.concatenate([wq_a, wq_b], axis=1).astype(BF16)
    w_ukv = p["w_ukv"]
    wk_t = jnp.transpose(w_ukv[..., :QK_NOPE], (1, 2, 0))
    eye = jnp.eye(QK_ROPE, dtype=F32)
    rope_rows = jnp.concatenate([jnp.zeros((QK_ROPE, KV_LORA), F32), eye, eye,
                                 jnp.zeros((QK_ROPE, LANES - 2 * QK_ROPE), F32)], axis=1)
    wk = jnp.concatenate([
        jnp.concatenate([wk_t, jnp.zeros((N_HEADS, QK_NOPE, LANES), F32)], axis=2),
        jnp.broadcast_to(rope_rows, (N_HEADS, QK_ROPE, KV_LORA + LANES)),
        jnp.zeros((N_HEADS, padq, KV_LORA + LANES), F32)], axis=1)
    f["w_k"] = wk.reshape(N_HEADS * HEAD_PAD, KV_LORA + LANES).astype(BF16)
    w_v = jnp.concatenate([w_ukv[..., QK_NOPE:], jnp.zeros((KV_LORA, N_HEADS, HEAD_PAD - V_DIM), F32)], axis=-1)
    f["w_v"] = w_v.reshape(KV_LORA, N_HEADS * HEAD_PAD).astype(BF16)
    f["v_one"] = (jnp.arange(N_HEADS * HEAD_PAD) % HEAD_PAD == V_DIM).astype(F32).reshape(1, -1)

    scale = (QK_NOPE + QK_ROPE) ** -0.5 * math.log2(math.e)
    f["rope"] = {}
    for s in sorted(set(seq_lens)):
        inv_freq = ROPE_THETA ** (-jnp.arange(0, QK_ROPE, 2, dtype=F32) / QK_ROPE)
        ang = jnp.arange(s, dtype=F32)[:, None] * inv_freq[None, :]
        cos, sin = jnp.cos(ang), jnp.sin(ang)
        one, zero = jnp.ones((s, QK_NOPE), F32), jnp.zeros((s, QK_NOPE), F32)
        zp = jnp.zeros((s, padq), F32)
        cq_t = scale * jnp.concatenate([one, cos, cos, zp], axis=1)
        sq_t = scale * jnp.concatenate([zero, sin, sin, zp], axis=1)
        tk_t = jnp.concatenate([cos, cos, sin, sin, jnp.zeros((s, LANES - 2 * QK_ROPE), F32)], axis=1)
        f["rope"][s] = (cq_t, sq_t, tk_t)

    f["conv_w"] = jnp.concatenate([p["conv_w"], jnp.zeros((8 - CONV_WIDTH, CONV_DIM), F32)], axis=0)
    f["conv_b"] = p["conv_b"].reshape(1, CONV_DIM)
    pad_h = jnp.zeros((LANES - 2 * SSD_HEADS,), F32)
    f["dt_bias"] = jnp.concatenate([p["dt_bias_f"], p["dt_bias_b"], pad_h]).reshape(1, LANES)
    f["a_log"] = jnp.concatenate([p["a_log_f"], p["a_log_b"], pad_h]).reshape(1, LANES)
    head_of_lane = jnp.arange(D_INNER, dtype=jnp.int32) // SSD_HEADDIM
    lane_id = jnp.arange(LANES, dtype=jnp.int32)[:, None]
    f["expand_f"] = (lane_id == head_of_lane[None, :]).astype(BF16)
    f["expand_b"] = (lane_id == head_of_lane[None, :] + SSD_HEADS).astype(BF16)
    f["d_skip"] = jnp.repeat(p["d_skip"], SSD_HEADDIM).reshape(1, D_INNER)
    f["ssd_nw"] = p["ssd_norm_w"].reshape(1, D_INNER)
    f["w_oa"] = p["w_o_attn"].astype(BF16)
    f["w_os"] = p["w_o_ssd"].astype(BF16)
    f["w_out"] = p["w_out"].astype(BF16)
    f["ln1_g"] = p["ln1_g"].reshape(1, D_MODEL)
    f["ln1_b"] = p["ln1_b"].reshape(1, D_MODEL)

    wr_t = p["w_router"].T
    wr_hi = wr_t.astype(BF16)
    f["wr_hi"] = wr_hi
    f["wr_lo"] = (wr_t - wr_hi.astype(F32)).astype(BF16)
    f["r_bias"] = p["router_bias"].reshape(N_EXPERTS, 1)
    f["w_gate"] = p["w_gate"].astype(BF16)
    f["w_up"] = p["w_up"].astype(BF16)
    f["w_down"] = p["w_down"].astype(BF16)
    f["ws_gate"] = p["ws_gate"].astype(BF16)
    f["ws_up"] = p["ws_up"].astype(BF16)
    f["ws_down"] = p["ws_down"].astype(BF16)
    f["ln2_g"] = p["ln2_g"].reshape(1, D_MODEL)
    f["ln2_b"] = p["ln2_b"].reshape(1, D_MODEL)
    return f


ROUTER_TILE = 512


def _layer(x, f):
    b, s, d = x.shape
    t = b * s
    x2d = x.reshape(t, d)
    qkv, z, xbc, dt, g = _in_proj(x2d, f["wq"], f["wz"], f["wx"], f["wdt"], f["wg"])

    cq_t, sq_t, tk_t = f["rope"][s]
    q, kt, v = _mla_prep(qkv.reshape(b, s, QKV_W), f["qn"], f["kvn"], f["w_q"], f["w_k"], f["w_v"], f["v_one"],
                         cq_t, sq_t, tk_t)
    o = _attention(q, kt, v)

    conv_out = _conv(xbc.reshape(b, s, CONV_DIM), f["conv_w"], f["conv_b"])
    dt3 = dt.reshape(b, s, LANES)
    y_f = _ssd(conv_out, dt3, f["dt_bias"], f["a_log"], f["expand_f"], reverse=False)
    y_b = _ssd(conv_out, dt3, f["dt_bias"], f["a_log"], f["expand_b"], reverse=True)

    x1 = _mix(x2d, o.reshape(t, -1), y_f.reshape(t, -1), y_b.reshape(t, -1), conv_out.reshape(t, -1), z, g,
              f["w_oa"], f["w_os"], f["w_out"], f["d_skip"], f["ssd_nw"], f["ln1_g"], f["ln1_b"])

    su = (jnp.arange(ROUTER_TILE)[:, None] < jnp.arange(ROUTER_TILE)[None, :]).astype(BF16)
    idx, wts, pos, cnt = _router(x1, f["wr_hi"], f["wr_lo"], f["r_bias"], su, tm=ROUTER_TILE)

    counts = cnt[:, 0].astype(jnp.int32)
    padded = (counts + EXPERT_BLOCK - 1) // EXPERT_BLOCK * EXPERT_BLOCK
    pad_end = jnp.cumsum(padded)
    pad_start = pad_end - padded
    nb = t * TOP_K // EXPERT_BLOCK + N_EXPERTS
    n_used = pad_end[-1] // EXPERT_BLOCK
    blk_row = jnp.minimum(jnp.arange(nb, dtype=jnp.int32), n_used - 1) * EXPERT_BLOCK
    block_e = jnp.minimum(jnp.sum((blk_row[:, None] >= pad_end[None, :]).astype(jnp.int32), axis=1), N_EXPERTS - 1)

    dest_flat = _dest_rows(pad_start, idx, pos).T.reshape(t * TOP_K)
    xs = _scatter(pad_start, padded, n_used.reshape(1), dest_flat, x1, nb * EXPERT_BLOCK)
    ys = _experts(block_e, xs, f["w_gate"], f["w_up"], f["w_down"])
    y = _final(dest_flat, x1, wts.T, f["ws_gate"], f["ws_up"], f["ws_down"], f["ln2_g"], f["ln2_b"], ys)
    return y.reshape(b, s, d)


def kernel(x_prompt, x_sample, w_in, q_norm_w, kv_norm_w, w_uq, w_ukv, w_o_attn, conv_w, conv_b, dt_bias_f,
           dt_bias_b, a_log_f, a_log_b, d_skip, ssd_norm_w, w_o_ssd, w_out, ln1_g, ln1_b, w_router, router_bias,
           w_gate, w_up, w_down, ws_gate, ws_up, ws_down, ln2_g, ln2_b):
    params = dict(w_in=w_in, q_norm_w=q_norm_w, kv_norm_w=kv_norm_w, w_uq=w_uq, w_ukv=w_ukv, w_o_attn=w_o_attn,
                  conv_w=conv_w, conv_b=conv_b, dt_bias_f=dt_bias_f, dt_bias_b=dt_bias_b, a_log_f=a_log_f,
                  a_log_b=a_log_b, d_skip=d_skip, ssd_norm_w=ssd_norm_w, w_o_ssd=w_o_ssd, w_out=w_out,
                  ln1_g=ln1_g, ln1_b=ln1_b, w_router=w_router, router_bias=router_bias, w_gate=w_gate,
                  w_up=w_up, w_down=w_down, ws_gate=ws_gate, ws_up=ws_up, ws_down=ws_down, ln2_g=ln2_g,
                  ln2_b=ln2_b)
    assert w_in.shape[0] == DEPTH == 1
    params = {k: v[0] for k, v in params.items()}
    f = _prepare(params, (x_prompt.shape[1], x_sample.shape[1]))
    return (_layer(x_prompt, f), _layer(x_sample, f))
```

```python
import functools
import math

import jax
import jax.numpy as jnp
import numpy as np
from jax import lax
from jax.experimental import pallas as pl
from jax.experimental.pallas import tpu as pltpu

F32 = jnp.float32
BF16 = jnp.bfloat16

D_MODEL = 1024
DEPTH = 1
N_HEADS = 16
QK_NOPE = 64
QK_ROPE = 32
V_DIM = 64
Q_LORA = 384
KV_LORA = 256
ROPE_THETA = 10000.0
D_INNER = 2 * D_MODEL
SSD_HEADDIM = 64
SSD_HEADS = D_INNER // SSD_HEADDIM
SSD_GROUPS = 4
D_STATE = 128
CONV_WIDTH = 5
CONV_DIM = D_INNER + 2 * SSD_GROUPS * D_STATE
CHUNK = 128
N_EXPERTS = 64
TOP_K = 8
N_EXPERT_GROUPS = 8
TOPK_GROUPS = 4
D_EXPERT = D_MODEL // 4
D_SHARED = D_MODEL // 4
ROUTED_SCALE = 2.5
LN_EPS = 1e-5
RMS_EPS = 1e-6
DN_ALPHA = (2 * DEPTH) ** 0.25

LANES = 128
HEAD_PAD = 128
QKV_W = 768
KROPE_OFF = Q_LORA + KV_LORA
EXPERT_BLOCK = 512
PACKED_W = D_MODEL // 2
VMEM_LIMIT = 56 * 1024 * 1024


def _cparams(sem, vmem=VMEM_LIMIT):
    return pltpu.CompilerParams(dimension_semantics=sem, vmem_limit_bytes=vmem)


def _dot(a, b):
    return jnp.dot(a, b, preferred_element_type=F32)


def _dot_nt(a, b):
    return lax.dot_general(a, b, (((1,), (1,)), ((), ())), preferred_element_type=F32)


def _split3(a):
    h = a.astype(BF16)
    r = a - h.astype(F32)
    m = r.astype(BF16)
    l = (r - m.astype(F32)).astype(BF16)
    return h, m, l


def _dot_split_lhs(a_f32, b_bf16):
    h, m, l = _split3(a_f32)
    return _dot(h, b_bf16) + _dot(m, b_bf16) + _dot(l, b_bf16)


def _silu(x):
    return x * jax.nn.sigmoid(x)


def _pack_rows(x):
    n = x.shape[1] // 2
    hi = lax.bitcast_convert_type(x[:, :n].astype(BF16).astype(F32), jnp.uint32)
    lo = lax.bitcast_convert_type(x[:, n:].astype(BF16).astype(F32), jnp.uint32)
    return hi | (lo >> 16)


def _unpack_rows(w):
    hi = lax.bitcast_convert_type(w & jnp.uint32(0xFFFF0000), F32)
    lo = lax.bitcast_convert_type(w << 16, F32)
    return hi, lo


def _layer_norm_rows(r, g, b):
    mu = jnp.mean(r, axis=-1, keepdims=True)
    d = r - mu
    var = jnp.mean(d * d, axis=-1, keepdims=True)
    return d * lax.rsqrt(var + LN_EPS) * g + b


def _inproj_body(x_ref, wq_ref, wz_ref, wx_ref, wdt_ref, wg_ref,
                 qkv_ref, z_ref, xbc_ref, dt_ref, g_ref):
    xb = x_ref[...].astype(BF16)
    qkv_ref[...] = _dot(xb, wq_ref[...]).astype(BF16)
    z_ref[...] = _silu(_dot(xb, wz_ref[...])).astype(BF16)
    xbc_ref[...] = _dot(xb, wx_ref[...]).astype(BF16)
    dt_ref[...] = _dot(xb, wdt_ref[...])
    g_ref[...] = jax.nn.sigmoid(_dot(xb, wg_ref[...])).astype(BF16)


def _in_proj(x2d, wq, wz, wx, wdt, wg, tm=512):
    t = x2d.shape[0]
    const = lambda w: pl.BlockSpec(w.shape, lambda i: (0, 0), pipeline_mode=pl.Buffered(1))
    row = lambda n: pl.BlockSpec((tm, n), lambda i: (i, 0))
    return pl.pallas_call(
        _inproj_body,
        grid=(t // tm,),
        in_specs=[row(D_MODEL), const(wq), const(wz), const(wx), const(wdt), const(wg)],
        out_specs=[row(QKV_W), row(D_INNER), row(CONV_DIM), row(LANES), row(2 * D_MODEL)],
        out_shape=[jax.ShapeDtypeStruct((t, QKV_W), BF16),
                   jax.ShapeDtypeStruct((t, D_INNER), BF16),
                   jax.ShapeDtypeStruct((t, CONV_DIM), BF16),
                   jax.ShapeDtypeStruct((t, LANES), F32),
                   jax.ShapeDtypeStruct((t, 2 * D_MODEL), BF16)],
        compiler_params=_cparams(("parallel",)),
        name="in_proj",
    )(x2d, wq, wz, wx, wdt, wg)


def _mla_prep_body(qkv_ref, qn_ref, kvn_ref, wq_ref, wk_ref, wv_ref, vone_ref, cq_ref, sq_ref, tk_ref,
                   q_ref, kt_ref, v_ref):
    qkv = qkv_ref[0].astype(F32)
    cq = qkv[:, :Q_LORA]
    cq = cq * lax.rsqrt(jnp.mean(cq * cq, axis=-1, keepdims=True) + RMS_EPS) * qn_ref[...]
    ckv = qkv[:, Q_LORA:KROPE_OFF]
    ckv = ckv * lax.rsqrt(jnp.mean(ckv * ckv, axis=-1, keepdims=True) + RMS_EPS) * kvn_ref[...]
    ckv_b = ckv.astype(BF16)
    qq = _dot(cq.astype(BF16), wq_ref[...])
    half = N_HEADS * HEAD_PAD
    cq_t = cq_ref[...]
    sq_t = sq_ref[...]
    for h in range(N_HEADS):
        lo = h * HEAD_PAD
        q_ref[0, :, lo:lo + HEAD_PAD] = (qq[:, lo:lo + HEAD_PAD] * cq_t
                                          + qq[:, half + lo:half + lo + HEAD_PAD] * sq_t).astype(BF16)
    kr = (qkv[:, KROPE_OFF:] * tk_ref[...]).astype(BF16)
    e = jnp.concatenate([ckv_b, kr], axis=1)
    kt_ref[0] = _dot_nt(wk_ref[...], e).astype(BF16)
    v_ref[0] = (_dot(ckv_b, wv_ref[...]) + vone_ref[...]).astype(BF16)


def _mla_prep(qkv, qn, kvn, wq, wk, wv, vone, cq_t, sq_t, tk_t, ts=512):
    b, s, _ = qkv.shape
    const = lambda w: pl.BlockSpec(w.shape, lambda bi, i: (0, 0))
    tab = pl.BlockSpec((ts, LANES), lambda bi, i: (i, 0))
    hw = N_HEADS * HEAD_PAD
    return pl.pallas_call(
        _mla_prep_body,
        grid=(b, s // ts),
        in_specs=[pl.BlockSpec((1, ts, QKV_W), lambda bi, i: (bi, i, 0)),
                  const(qn), const(kvn), const(wq), const(wk), const(wv), const(vone), tab, tab, tab],
        out_specs=[pl.BlockSpec((1, ts, hw), lambda bi, i: (bi, i, 0)),
                   pl.BlockSpec((1, hw, ts), lambda bi, i: (bi, 0, i)),
                   pl.BlockSpec((1, ts, hw), lambda bi, i: (bi, i, 0))],
        out_shape=[jax.ShapeDtypeStruct((b, s, hw), BF16),
                   jax.ShapeDtypeStruct((b, hw, s), BF16),
                   jax.ShapeDtypeStruct((b, s, hw), BF16)],
        compiler_params=_cparams(("parallel", "parallel")),
        name="mla_prep",
    )(qkv, qn, kvn, wq, wk, wv, vone, cq_t, sq_t, tk_t)


def _attn_body(q_ref, kt_ref, v_ref, o_ref):
    tq = q_ref.shape[1]
    for r in range(tq // ATTN_ROWS):
        rows = slice(r * ATTN_ROWS, (r + 1) * ATTN_ROWS)
        outs = []
        for hh in range(2):
            sl = slice(hh * HEAD_PAD, (hh + 1) * HEAD_PAD)
            s = _dot(q_ref[0, rows, sl], kt_ref[0, sl, :])
            m = jnp.max(s, axis=-1, keepdims=True)
            p = jnp.exp2(s - m).astype(BF16)
            o = _dot(p, v_ref[0, :, sl])
            outs.append(o / o[:, V_DIM:V_DIM + 1])
        lane = lax.broadcasted_iota(jnp.int32, outs[0].shape, 1)
        o_ref[0, rows, :] = jnp.where(lane < V_DIM, outs[0], pltpu.roll(outs[1], V_DIM, axis=1)).astype(BF16)


ATTN_ROWS = 256


def _attention(q, kt, v, tq=1024):
    b, s, _ = q.shape
    pairs = N_HEADS // 2
    return pl.pallas_call(
        _attn_body,
        grid=(b, pairs, s // tq),
        in_specs=[pl.BlockSpec((1, tq, 2 * HEAD_PAD), lambda bi, j, i: (bi, i, j)),
                  pl.BlockSpec((1, 2 * HEAD_PAD, s), lambda bi, j, i: (bi, j, 0)),
                  pl.BlockSpec((1, s, 2 * HEAD_PAD), lambda bi, j, i: (bi, 0, j))],
        out_specs=pl.BlockSpec((1, tq, 2 * V_DIM), lambda bi, j, i: (bi, i, j)),
        out_shape=jax.ShapeDtypeStruct((b, s, N_HEADS * V_DIM), BF16),
        compiler_params=_cparams(("parallel", "parallel", "arbitrary")),
        name="attention",
    )(q, kt, v)


CONV_HALO = 16


def _conv_body(prev_ref, main_ref, next_ref, w_ref, b_ref, o_ref, ext_ref):
    i = pl.program_id(1)
    n = pl.num_programs(1)
    ts = main_ref.shape[1]
    prev = prev_ref[0].astype(F32)
    nxt = next_ref[0].astype(F32)
    ext_ref[0:CONV_HALO, :] = jnp.where(i > 0, prev, 0.0)
    ext_ref[CONV_HALO:CONV_HALO + ts, :] = main_ref[0].astype(F32)
    ext_ref[CONV_HALO + ts:, :] = jnp.where(i < n - 1, nxt, 0.0)
    acc = b_ref[...]
    for k in range(CONV_WIDTH):
        off = CONV_HALO - CONV_WIDTH // 2 + k
        acc = acc + ext_ref[off:off + ts, :] * w_ref[k:k + 1, :]
    o_ref[0] = _silu(acc).astype(BF16)


def _conv(xbc, w8, bias, ts=512, tc=512):
    b, s, c = xbc.shape
    r = ts // CONV_HALO
    last = s // CONV_HALO - 1
    return pl.pallas_call(
        _conv_body,
        grid=(b, s // ts, c // tc),
        in_specs=[pl.BlockSpec((1, CONV_HALO, tc), lambda bi, i, j: (bi, jnp.maximum(i * r - 1, 0), j)),
                  pl.BlockSpec((1, ts, tc), lambda bi, i, j: (bi, i, j)),
                  pl.BlockSpec((1, CONV_HALO, tc), lambda bi, i, j: (bi, jnp.minimum((i + 1) * r, last), j)),
                  pl.BlockSpec((8, tc), lambda bi, i, j: (0, j)),
                  pl.BlockSpec((1, tc), lambda bi, i, j: (0, j))],
        out_specs=pl.BlockSpec((1, ts, tc), lambda bi, i, j: (bi, i, j)),
        out_shape=jax.ShapeDtypeStruct((b, s, c), BF16),
        scratch_shapes=[pltpu.VMEM((ts + 2 * CONV_HALO, tc), F32)],
        compiler_params=_cparams(("parallel", "parallel", "parallel")),
        name="conv",
    )(xbc, xbc, xbc, w8, bias)


def _ssd_body(reverse, xs_ref, b_ref, c_ref, dt_ref, dtb_ref, alog_ref, e_ref, y_ref, state_ref):
    @pl.when(pl.program_id(1) == 0)
    def _():
        state_ref[...] = jnp.zeros_like(state_ref)

    q = CHUNK
    lane0 = SSD_HEADS if reverse else 0
    rows = lax.broadcasted_iota(jnp.int32, (q, q), 0)
    cols = lax.broadcasted_iota(jnp.int32, (q, q), 1)
    mask = (rows <= cols) if reverse else (rows >= cols)
    tri = mask.astype(BF16)

    x = dt_ref[0] + dtb_ref[...]
    dt = jnp.maximum(x, 0.0) + jnp.log(1.0 + jnp.exp(-jnp.abs(x)))
    da = dt * (-jnp.exp(alog_ref[...]))
    cum = _dot_split_lhs_rhs(tri, da)
    cum_t = cum.T
    dt_t = dt.T
    total = cum[0:1, :] if reverse else cum[q - 1:q, :]
    w = dt * jnp.exp(total - cum)
    w_exp = _dot(w.astype(BF16), e_ref[...])
    xs = xs_ref[0]
    x_dec = (xs.astype(F32) * w_exp).astype(BF16)

    bm = b_ref[0].astype(F32)
    cm = c_ref[0]
    cm_f = cm.astype(F32)
    state_old = state_ref[...].astype(BF16)
    gw = D_STATE
    hp = SSD_HEADS // SSD_GROUPS * SSD_HEADDIM
    new_states = []
    for g in range(SSD_GROUPS):
        bt = bm[:, g * gw:(g + 1) * gw].T.astype(BF16)
        new_states.append(_dot(bt, x_dec[:, g * hp:(g + 1) * hp]))
        cb = _dot(cm[:, g * gw:(g + 1) * gw], bt)
        c_g = cm_f[:, g * gw:(g + 1) * gw]
        for jj in range(hp // LANES):
            j = g * (hp // LANES) + jj
            sl = slice(j * LANES, (j + 1) * LANES)
            rhs = jnp.concatenate([xs[:, sl], state_old[:, sl]], axis=0)
            ys = []
            for hh in range(2):
                ln = lane0 + 2 * j + hh
                col = jnp.broadcast_to(cum[:, ln:ln + 1], (q, q))
                seg = col - cum_t[ln:ln + 1, :]
                decay = jnp.exp(jnp.where(mask, seg, -jnp.inf))
                m = cb * decay * dt_t[ln:ln + 1, :]
                cs = c_g * jnp.exp(col)
                lhs = jnp.concatenate([m.astype(BF16), cs.astype(BF16)], axis=1)
                ys.append(_dot(lhs, rhs))
            lane = lax.broadcasted_iota(jnp.int32, (q, LANES), 1)
            y_ref[0, :, sl] = jnp.where(lane < SSD_HEADDIM, ys[0], ys[1]).astype(BF16)

    dec = jnp.broadcast_to(jnp.exp(total), (8, LANES))
    dec_exp = _dot_split_lhs(dec, e_ref[...])[0:1, :]
    state_ref[...] = state_ref[...] * dec_exp + jnp.concatenate(new_states, axis=1)


def _dot_split_lhs_rhs(tri_bf16, da_f32):
    h, m, l = _split3(da_f32)
    return _dot(tri_bf16, h) + _dot(tri_bf16, m) + _dot(tri_bf16, l)


def _ssd(conv_out, dt, dt_bias, a_log, expand, reverse):
    b, s, _ = conv_out.shape
    nc = s // CHUNK
    cidx = (lambda c: nc - 1 - c) if reverse else (lambda c: c)
    bcol = D_INNER // (SSD_GROUPS * D_STATE)
    gn = SSD_GROUPS * D_STATE
    const = lambda w: pl.BlockSpec(w.shape, lambda bi, c: (0, 0))
    return pl.pallas_call(
        functools.partial(_ssd_body, reverse),
        grid=(b, nc),
        in_specs=[pl.BlockSpec((1, CHUNK, D_INNER), lambda bi, c: (bi, cidx(c), 0)),
                  pl.BlockSpec((1, CHUNK, gn), lambda bi, c: (bi, cidx(c), bcol)),
                  pl.BlockSpec((1, CHUNK, gn), lambda bi, c: (bi, cidx(c), bcol + 1)),
                  pl.BlockSpec((1, CHUNK, LANES), lambda bi, c: (bi, cidx(c), 0)),
                  const(dt_bias), const(a_log), const(expand)],
        out_specs=pl.BlockSpec((1, CHUNK, D_INNER), lambda bi, c: (bi, cidx(c), 0)),
        out_shape=jax.ShapeDtypeStruct((b, s, D_INNER), BF16),
        scratch_shapes=[pltpu.VMEM((D_STATE, D_INNER), F32)],
        compiler_params=_cparams(("parallel", "arbitrary")),
        name="ssd_bwd" if reverse else "ssd_fwd",
    )(conv_out, conv_out, conv_out, dt, dt_bias, a_log, expand)


def _mix_body(x_ref, o_ref, yf_ref, yb_ref, xs_ref, z_ref, g_ref,
              woa_ref, wos_ref, wout_ref, dsk_ref, nw_ref, lg_ref, lb_ref, x1_ref, x1p_ref):
    attn = _dot(o_ref[...], woa_ref[...])
    y = yf_ref[...].astype(F32) + yb_ref[...].astype(F32) + dsk_ref[...] * xs_ref[...].astype(F32)
    y = y * z_ref[...].astype(F32)
    yn = y * lax.rsqrt(jnp.mean(y * y, axis=-1, keepdims=True) + RMS_EPS) * nw_ref[...]
    ssd = _dot(yn.astype(BF16), wos_ref[...])
    g = g_ref[...].astype(F32)
    mixed_in = g[:, :D_MODEL] * attn + g[:, D_MODEL:] * ssd
    mixed = _dot(mixed_in.astype(BF16), wout_ref[...])
    x1 = _layer_norm_rows(DN_ALPHA * x_ref[...] + mixed, lg_ref[...], lb_ref[...])
    x1_ref[...] = x1
    x1p_ref[...] = _pack_rows(x1)


def _mix(x2d, o, yf, yb, conv_out, z, g, woa, wos, wout, dsk, nw, lg, lb, tm=256):
    t = x2d.shape[0]
    const = lambda w: pl.BlockSpec(w.shape, lambda i: (0, 0))
    row = lambda n: pl.BlockSpec((tm, n), lambda i: (i, 0))
    return pl.pallas_call(
        _mix_body,
        grid=(t // tm,),
        in_specs=[row(D_MODEL), row(D_MODEL), row(D_INNER), row(D_INNER), row(D_INNER), row(D_INNER),
                  row(2 * D_MODEL), const(woa), const(wos), const(wout), const(dsk), const(nw),
                  const(lg), const(lb)],
        out_specs=[row(D_MODEL), row(PACKED_W)],
        out_shape=[jax.ShapeDtypeStruct((t, D_MODEL), F32), jax.ShapeDtypeStruct((t, PACKED_W), jnp.uint32)],
        compiler_params=_cparams(("parallel",)),
        name="mix_ln1",
    )(x2d, o, yf, yb, conv_out, z, g, woa, wos, wout, dsk, nw, lg, lb)


def _router_body(x_ref, wh_ref, wl_ref, bias_ref, su_ref, idx_ref, wts_ref, pos_ref, cnt_ref, carry_ref):
    @pl.when(pl.program_id(0) == 0)
    def _():
        carry_ref[...] = jnp.zeros_like(carry_ref)

    tm = x_ref.shape[0]
    x = x_ref[...]
    xh = x.astype(BF16)
    xl = (x - xh.astype(F32)).astype(BF16)
    wh = wh_ref[...]
    logits = _dot_nt(wh, xh) + _dot_nt(wh, xl) + _dot_nt(wl_ref[...], xh)
    scores = jax.nn.sigmoid(logits)
    choice = scores + bias_ref[...]
    gsz = N_EXPERTS // N_EXPERT_GROUPS
    shp = (N_EXPERT_GROUPS, gsz, tm)
    ch = choice.reshape(shp)
    sc = scores.reshape(shp)
    jio = lax.broadcasted_iota(jnp.int32, shp, 1)
    gio = lax.broadcasted_iota(jnp.int32, shp, 0)
    eio = gio * gsz + jio
    neg = -jnp.inf

    m1 = jnp.max(ch, axis=1, keepdims=True)
    i1 = jnp.min(jnp.where(ch == m1, jio, gsz), axis=1, keepdims=True)
    m2 = jnp.max(jnp.where(jio == i1, neg, ch), axis=1, keepdims=True)
    gs = m1 + m2
    gio1 = lax.broadcasted_iota(jnp.int32, (N_EXPERT_GROUPS, 1, tm), 0)
    sel = jnp.zeros(gs.shape, jnp.bool_)
    cur = gs
    for _ in range(TOPK_GROUPS):
        m = jnp.max(cur, axis=0, keepdims=True)
        gi = jnp.min(jnp.where(cur == m, gio1, N_EXPERT_GROUPS), axis=0, keepdims=True)
        hit = gio1 == gi
        sel = jnp.logical_or(sel, hit)
        cur = jnp.where(hit, neg, cur)
    masked = jnp.where(sel, ch, neg)

    def red(op, v):
        return op(op(v, axis=1, keepdims=True), axis=0, keepdims=True)

    hits, idxs, ws = [], [], []
    for _ in range(TOP_K):
        m = red(jnp.max, masked)
        ei = red(jnp.min, jnp.where(masked == m, eio, N_EXPERTS))
        hit = eio == ei
        hits.append(hit)
        idxs.append(ei)
        ws.append(red(jnp.sum, jnp.where(hit, sc, 0.0)))
        masked = jnp.where(hit, neg, masked)
    wsum = ws[0]
    for k in range(1, TOP_K):
        wsum = wsum + ws[k]

    onehot = hits[0]
    for k in range(1, TOP_K):
        onehot = jnp.logical_or(onehot, hits[k])
    oh = onehot.astype(F32).reshape(N_EXPERTS, tm)
    before = _dot(oh.astype(BF16), su_ref[...]) + carry_ref[...]
    before3 = before.reshape(shp)
    for k in range(TOP_K):
        idx_ref[k:k + 1, :] = idxs[k].reshape(1, tm)
        wts_ref[k:k + 1, :] = (ws[k] / wsum * ROUTED_SCALE).reshape(1, tm)
        pos_ref[k:k + 1, :] = red(jnp.sum, jnp.where(hits[k], before3, 0.0)).reshape(1, tm).astype(jnp.int32)
    carry_ref[...] = carry_ref[...] + jnp.sum(oh, axis=1, keepdims=True)
    cnt_ref[...] = jnp.broadcast_to(carry_ref[...], cnt_ref.shape)


def _router(x1, wr_hi, wr_lo, bias, su, tm=512):
    t = x1.shape[0]
    const = lambda w: pl.BlockSpec(w.shape, lambda i: (0, 0))
    out_tok = pl.BlockSpec((TOP_K, tm), lambda i: (0, i))
    return pl.pallas_call(
        _router_body,
        grid=(t // tm,),
        in_specs=[pl.BlockSpec((tm, D_MODEL), lambda i: (i, 0)), const(wr_hi), const(wr_lo), const(bias),
                  const(su)],
        out_specs=[out_tok, out_tok, out_tok, pl.BlockSpec((N_EXPERTS, LANES), lambda i: (0, 0))],
        out_shape=[jax.ShapeDtypeStruct((TOP_K, t), jnp.int32),
                   jax.ShapeDtypeStruct((TOP_K, t), F32),
                   jax.ShapeDtypeStruct((TOP_K, t), jnp.int32),
                   jax.ShapeDtypeStruct((N_EXPERTS, LANES), F32)],
        scratch_shapes=[pltpu.VMEM((N_EXPERTS, 1), F32)],
        compiler_params=_cparams(("arbitrary",)),
        name="router",
    )(x1, wr_hi, wr_lo, bias, su)


def _row_copy_wait(src_ref, dst_hbm, sem, n_rows, times):
    for _ in range(times):
        pltpu.make_async_copy(src_ref.at[pl.ds(0, n_rows)], dst_hbm.at[pl.ds(0, n_rows)], sem).wait()


def _scatter_body(ps_ref, pd_ref, nu_ref, dest_ref, x_ref, out_hbm, zero_ref, sem, zsem):
    tm = x_ref.shape[0]
    nb = out_hbm.shape[0] // EXPERT_BLOCK

    @pl.when(pl.program_id(0) == 0)
    def _():
        zero_ref[...] = jnp.zeros_like(zero_ref)

        def zcopy(row):
            start = pl.multiple_of(row, EXPERT_BLOCK)
            return pltpu.make_async_copy(zero_ref, out_hbm.at[pl.ds(start, EXPERT_BLOCK)], zsem)

        def last_block(e):
            return ps_ref[e] + pd_ref[e] - EXPERT_BLOCK

        def zstart(e, c):
            @pl.when(pd_ref[e] > 0)
            def _():
                zcopy(last_block(e)).start()
            return c

        def zwait(e, c):
            @pl.when(pd_ref[e] > 0)
            def _():
                zcopy(last_block(e)).wait()
            return c

        def tstart(b, c):
            zcopy(b * EXPERT_BLOCK).start()
            return c

        def twait(b, c):
            zcopy(b * EXPERT_BLOCK).wait()
            return c

        lax.fori_loop(0, N_EXPERTS, zstart, 0)
        lax.fori_loop(nu_ref[0], nb, tstart, 0)
        lax.fori_loop(0, N_EXPERTS, zwait, 0)
        lax.fori_loop(nu_ref[0], nb, twait, 0)

    def issue(t, c):
        for k in range(TOP_K):
            d = dest_ref[t * TOP_K + k]
            pltpu.make_async_copy(x_ref.at[pl.ds(t, 1)], out_hbm.at[pl.ds(d, 1)], sem).start()
        return c

    lax.fori_loop(0, tm, issue, 0)
    _row_copy_wait(x_ref, out_hbm, sem, tm, TOP_K)


def _scatter(pad_start, padded, n_used, dest_flat, x1, rows, tm=256):
    t = x1.shape[0]
    smem_tok = pl.BlockSpec((tm * TOP_K,), lambda i, ps, pd, nu: (i,), memory_space=pltpu.SMEM)
    grid_spec = pltpu.PrefetchScalarGridSpec(
        num_scalar_prefetch=3,
        grid=(t // tm,),
        in_specs=[smem_tok, pl.BlockSpec((tm, PACKED_W), lambda i, ps, pd, nu: (i, 0))],
        out_specs=pl.BlockSpec(memory_space=pl.ANY),
        scratch_shapes=[pltpu.VMEM((EXPERT_BLOCK, PACKED_W), jnp.uint32),
                        pltpu.SemaphoreType.DMA(()), pltpu.SemaphoreType.DMA(())],
    )
    return pl.pallas_call(
        _scatter_body,
        grid_spec=grid_spec,
        out_shape=jax.ShapeDtypeStruct((rows, PACKED_W), jnp.uint32),
        compiler_params=_cparams(("arbitrary",)),
        name="moe_scatter",
    )(pad_start, padded, n_used, dest_flat, x1)


def _expert_body(be_ref, x_ref, wg_ref, wu_ref, wd_ref, y_ref):
    hi, lo = _unpack_rows(x_ref[...])
    xb = jnp.concatenate([hi.astype(BF16), lo.astype(BF16)], axis=1)
    h = _silu(_dot(xb, wg_ref[0])) * _dot(xb, wu_ref[0])
    y_ref[...] = _pack_rows(_dot(h.astype(BF16), wd_ref[0]))


def _experts(block_e, xs, wg, wu, wd):
    rows = xs.shape[0]
    nb = rows // EXPERT_BLOCK
    grid_spec = pltpu.PrefetchScalarGridSpec(
        num_scalar_prefetch=1,
        grid=(nb,),
        in_specs=[pl.BlockSpec((EXPERT_BLOCK, PACKED_W), lambda b, be: (b, 0)),
                  pl.BlockSpec((1, D_MODEL, D_EXPERT), lambda b, be: (be[b], 0, 0)),
                  pl.BlockSpec((1, D_MODEL, D_EXPERT), lambda b, be: (be[b], 0, 0)),
                  pl.BlockSpec((1, D_EXPERT, D_MODEL), lambda b, be: (be[b], 0, 0))],
        out_specs=pl.BlockSpec((EXPERT_BLOCK, PACKED_W), lambda b, be: (b, 0)),
    )
    return pl.pallas_call(
        _expert_body,
        grid_spec=grid_spec,
        out_shape=jax.ShapeDtypeStruct((rows, PACKED_W), jnp.uint32),
        compiler_params=_cparams(("parallel",)),
        name="moe_experts",
    )(block_e, xs, wg, wu, wd)


def _final_body(dest_ref, x1_ref, wt_ref, wsg_ref, wsu_ref, wsd_ref, lg_ref, lb_ref,
                ys_hbm, out_ref, gbuf_ref, sem):
    tm = x1_ref.shape[0]

    def issue(t, c):
        for k in range(TOP_K):
            d = dest_ref[t * TOP_K + k]
            pltpu.make_async_copy(ys_hbm.at[pl.ds(d, 1)], gbuf_ref.at[k, pl.ds(t, 1)], sem).start()
        return c

    lax.fori_loop(0, tm, issue, 0)
    x1 = x1_ref[...]
    xb = x1.astype(BF16)
    h = _silu(_dot(xb, wsg_ref[...])) * _dot(xb, wsu_ref[...])
    acc = DN_ALPHA * x1 + _dot(h.astype(BF16), wsd_ref[...])
    for k in range(TOP_K):
        pltpu.make_async_copy(ys_hbm.at[pl.ds(0, tm)], gbuf_ref.at[k], sem).wait()
    wt = wt_ref[...]
    acc_hi, acc_lo = acc[:, :PACKED_W], acc[:, PACKED_W:]
    for k in range(TOP_K):
        hi, lo = _unpack_rows(gbuf_ref[k])
        acc_hi = acc_hi + wt[:, k:k + 1] * hi
        acc_lo = acc_lo + wt[:, k:k + 1] * lo
    acc = jnp.concatenate([acc_hi, acc_lo], axis=1)
    out_ref[...] = _layer_norm_rows(acc, lg_ref[...], lb_ref[...])


def _final(dest_flat, x1, wt_rows, wsg, wsu, wsd, lg, lb, ys, tm=128):
    t = x1.shape[0]
    const = lambda w: pl.BlockSpec(w.shape, lambda i: (0, 0))
    return pl.pallas_call(
        _final_body,
        grid=(t // tm,),
        in_specs=[pl.BlockSpec((tm * TOP_K,), lambda i: (i,), memory_space=pltpu.SMEM),
                  pl.BlockSpec((tm, D_MODEL), lambda i: (i, 0)),
                  pl.BlockSpec((tm, TOP_K), lambda i: (i, 0)),
                  const(wsg), const(wsu), const(wsd), const(lg), const(lb),
                  pl.BlockSpec(memory_space=pl.ANY)],
        out_specs=pl.BlockSpec((tm, D_MODEL), lambda i: (i, 0)),
        out_shape=jax.ShapeDtypeStruct((t, D_MODEL), F32),
        scratch_shapes=[pltpu.VMEM((TOP_K, tm, PACKED_W), jnp.uint32), pltpu.SemaphoreType.DMA(())],
        compiler_params=_cparams(("arbitrary",)),
        name="moe_combine_ln2",
    )(dest_flat, x1, wt_rows, wsg, wsu, wsd, lg, lb, ys)


def _dest_body(ps_ref, idx_ref, pos_ref, out_ref):
    idx = idx_ref[...]
    acc = pos_ref[...]
    for e in range(N_EXPERTS):
        acc = acc + jnp.where(idx == e, ps_ref[e], 0)
    out_ref[...] = acc


def _dest_rows(pad_start, idx, pos, tn=2048):
    t = idx.shape[1]
    tn = min(tn, t)
    tok = pl.BlockSpec((TOP_K, tn), lambda i, ps: (0, i))
    grid_spec = pltpu.PrefetchScalarGridSpec(
        num_scalar_prefetch=1, grid=(t // tn,), in_specs=[tok, tok], out_specs=tok)
    return pl.pallas_call(
        _dest_body,
        grid_spec=grid_spec,
        out_shape=jax.ShapeDtypeStruct((TOP_K, t), jnp.int32),
        compiler_params=_cparams(("parallel",)),
        name="moe_dest",
    )(pad_start, idx, pos)


def _rot_cols(w):
    half = w.shape[-1] // 2
    return jnp.concatenate([-w[..., half:], w[..., :half]], axis=-1)


def _prepare(p, seq_lens):
    f = {}
    w_in = p["w_in"]
    o = np.cumsum((0, Q_LORA, KV_LORA, QK_ROPE, D_INNER, CONV_DIM, SSD_HEADS, SSD_HEADS, D_MODEL, D_MODEL))
    seg = lambda i: w_in[:, o[i]:o[i + 1]]
    zeros = lambda n: jnp.zeros((D_MODEL, n), F32)
    f["wq"] = jnp.concatenate([seg(0), seg(1), seg(2), _rot_cols(seg(2)), zeros(QKV_W - KROPE_OFF - 2 * QK_ROPE)],
                              axis=1).astype(BF16)
    f["wz"] = seg(3).astype(BF16)
    f["wx"] = seg(4).astype(BF16)
    f["wdt"] = jnp.concatenate([seg(5), seg(6), zeros(LANES - 2 * SSD_HEADS)], axis=1).astype(BF16)
    f["wg"] = jnp.concatenate([seg(7), seg(8)], axis=1).astype(BF16)

    f["qn"] = p["q_norm_w"].reshape(1, Q_LORA)
    f["kvn"] = p["kv_norm_w"].reshape(1, KV_LORA)
    w_uq = p["w_uq"]
    nope, rope = w_uq[..., :QK_NOPE], w_uq[..., QK_NOPE:]
    padq = HEAD_PAD - QK_NOPE - QK_ROPE
    zq = lambda n: jnp.zeros((Q_LORA, N_HEADS, n), F32)
    wq_a = jnp.concatenate([nope, rope, zq(padq)], axis=-1).reshape(Q_LORA, N_HEADS * HEAD_PAD)
    wq_b = jnp.concatenate([zq(QK_NOPE), _rot_cols(rope), zq(padq)], axis=-1).reshape(Q_LORA, N_HEADS * HEAD_PAD)
    f["w_q"] = jnp.concatenate([wq_a, wq_b], axis=1).astype(BF16)
    w_ukv = p["w_ukv"]
    wk_t = jnp.transpose(w_ukv[..., :QK_NOPE], (1, 2, 0))
    eye = jnp.eye(QK_ROPE, dtype=F32)
    rope_rows = jnp.concatenate([jnp.zeros((QK_ROPE, KV_LORA), F32), eye, eye,
                                 jnp.zeros((QK_ROPE, LANES - 2 * QK_ROPE), F32)], axis=1)
    wk = jnp.concatenate([
        jnp.concatenate([wk_t, jnp.zeros((N_HEADS, QK_NOPE, LANES), F32)], axis=2),
        jnp.broadcast_to(rope_rows, (N_HEADS, QK_ROPE, KV_LORA + LANES)),
        jnp.zeros((N_HEADS, padq, KV_LORA + LANES), F32)], axis=1)
    f["w_k"] = wk.reshape(N_HEADS * HEAD_PAD, KV_LORA + LANES).astype(BF16)
    w_v = jnp.concatenate([w_ukv[..., QK_NOPE:], jnp.zeros((KV_LORA, N_HEADS, HEAD_PAD - V_DIM), F32)], axis=-1)
    f["w_v"] = w_v.reshape(KV_LORA, N_HEADS * HEAD_PAD).astype(BF16)
    f["v_one"] = (jnp.arange(N_HEADS * HEAD_PAD) % HEAD_PAD == V_DIM).astype(F32).reshape(1, -1)

    scale = (QK_NOPE + QK_ROPE) ** -0.5 * math.log2(math.e)
    f["rope"] = {}
    for s in sorted(set(seq_lens)):
        inv_freq = ROPE_THETA ** (-jnp.arange(0, QK_ROPE, 2, dtype=F32) / QK_ROPE)
        ang = jnp.arange(s, dtype=F32)[:, None] * inv_freq[None, :]
        cos, sin = jnp.cos(ang), jnp.sin(ang)
        one, zero = jnp.ones((s, QK_NOPE), F32), jnp.zeros((s, QK_NOPE), F32)
        zp = jnp.zeros((s, padq), F32)
        cq_t = scale * jnp.concatenate([one, cos, cos, zp], axis=1)
        sq_t = scale * jnp.concatenate([zero, sin, sin, zp], axis=1)
        tk_t = jnp.concatenate([cos, cos, sin, sin, jnp.zeros((s, LANES - 2 * QK_ROPE), F32)], axis=1)
        f["rope"][s] = (cq_t, sq_t, tk_t)

    f["conv_w"] = jnp.concatenate([p["conv_w"], jnp.zeros((8 - CONV_WIDTH, CONV_DIM), F32)], axis=0)
    f["conv_b"] = p["conv_b"].reshape(1, CONV_DIM)
    pad_h = jnp.zeros((LANES - 2 * SSD_HEADS,), F32)
    f["dt_bias"] = jnp.concatenate([p["dt_bias_f"], p["dt_bias_b"], pad_h]).reshape(1, LANES)
    f["a_log"] = jnp.concatenate([p["a_log_f"], p["a_log_b"], pad_h]).reshape(1, LANES)
    head_of_lane = jnp.arange(D_INNER, dtype=jnp.int32) // SSD_HEADDIM
    lane_id = jnp.arange(LANES, dtype=jnp.int32)[:, None]
    f["expand_f"] = (lane_id == head_of_lane[None, :]).astype(BF16)
    f["expand_b"] = (lane_id == head_of_lane[None, :] + SSD_HEADS).astype(BF16)
    f["d_skip"] = jnp.repeat(p["d_skip"], SSD_HEADDIM).reshape(1, D_INNER)
    f["ssd_nw"] = p["ssd_norm_w"].reshape(1, D_INNER)
    f["w_oa"] = p["w_o_attn"].astype(BF16)
    f["w_os"] = p["w_o_ssd"].astype(BF16)
    f["w_out"] = p["w_out"].astype(BF16)
    f["ln1_g"] = p["ln1_g"].reshape(1, D_MODEL)
    f["ln1_b"] = p["ln1_b"].reshape(1, D_MODEL)

    wr_t = p["w_router"].T
    wr_hi = wr_t.astype(BF16)
    f["wr_hi"] = wr_hi
    f["wr_lo"] = (wr_t - wr_hi.astype(F32)).astype(BF16)
    f["r_bias"] = p["router_bias"].reshape(N_EXPERTS, 1)
    f["w_gate"] = p["w_gate"].astype(BF16)
    f["w_up"] = p["w_up"].astype(BF16)
    f["w_down"] = p["w_down"].astype(BF16)
    f["ws_gate"] = p["ws_gate"].astype(BF16)
    f["ws_up"] = p["ws_up"].astype(BF16)
    f["ws_down"] = p["ws_down"].astype(BF16)
    f["ln2_g"] = p["ln2_g"].reshape(1, D_MODEL)
    f["ln2_b"] = p["ln2_b"].reshape(1, D_MODEL)
    return f


ROUTER_TILE = 512


def _layer(x, f):
    b, s, d = x.shape
    t = b * s
    x2d = x.reshape(t, d)
    qkv, z, xbc, dt, g = _in_proj(x2d, f["wq"], f["wz"], f["wx"], f["wdt"], f["wg"])

    cq_t, sq_t, tk_t = f["rope"][s]
    q, kt, v = _mla_prep(qkv.reshape(b, s, QKV_W), f["qn"], f["kvn"], f["w_q"], f["w_k"], f["w_v"], f["v_one"],
                         cq_t, sq_t, tk_t)
    o = _attention(q, kt, v)

    conv_out = _conv(xbc.reshape(b, s, CONV_DIM), f["conv_w"], f["conv_b"])
    dt3 = dt.reshape(b, s, LANES)
    y_f = _ssd(conv_out, dt3, f["dt_bias"], f["a_log"], f["expand_f"], reverse=False)
    y_b = _ssd(conv_out, dt3, f["dt_bias"], f["a_log"], f["expand_b"], reverse=True)

    x1, x1p = _mix(x2d, o.reshape(t, -1), y_f.reshape(t, -1), y_b.reshape(t, -1), conv_out.reshape(t, -1), z, g,
              f["w_oa"], f["w_os"], f["w_out"], f["d_skip"], f["ssd_nw"], f["ln1_g"], f["ln1_b"])

    su = (jnp.arange(ROUTER_TILE)[:, None] < jnp.arange(ROUTER_TILE)[None, :]).astype(BF16)
    idx, wts, pos, cnt = _router(x1, f["wr_hi"], f["wr_lo"], f["r_bias"], su, tm=ROUTER_TILE)

    counts = cnt[:, 0].astype(jnp.int32)
    padded = (counts + EXPERT_BLOCK - 1) // EXPERT_BLOCK * EXPERT_BLOCK
    pad_end = jnp.cumsum(padded)
    pad_start = pad_end - padded
    nb = t * TOP_K // EXPERT_BLOCK + N_EXPERTS
    n_used = pad_end[-1] // EXPERT_BLOCK
    blk_row = jnp.minimum(jnp.arange(nb, dtype=jnp.int32), n_used - 1) * EXPERT_BLOCK
    block_e = jnp.minimum(jnp.sum((blk_row[:, None] >= pad_end[None, :]).astype(jnp.int32), axis=1), N_EXPERTS - 1)

    dest_flat = _dest_rows(pad_start, idx, pos).T.reshape(t * TOP_K)
    xs = _scatter(pad_start, padded, n_used.reshape(1), dest_flat, x1p, nb * EXPERT_BLOCK)
    ys = _experts(block_e, xs, f["w_gate"], f["w_up"], f["w_down"])
    y = _final(dest_flat, x1, wts.T, f["ws_gate"], f["ws_up"], f["ws_down"], f["ln2_g"], f["ln2_b"], ys)
    return y.reshape(b, s, d)


def kernel(x_prompt, x_sample, w_in, q_norm_w, kv_norm_w, w_uq, w_ukv, w_o_attn, conv_w, conv_b, dt_bias_f,
           dt_bias_b, a_log_f, a_log_b, d_skip, ssd_norm_w, w_o_ssd, w_out, ln1_g, ln1_b, w_router, router_bias,
           w_gate, w_up, w_down, ws_gate, ws_up, ws_down, ln2_g, ln2_b):
    params = dict(w_in=w_in, q_norm_w=q_norm_w, kv_norm_w=kv_norm_w, w_uq=w_uq, w_ukv=w_ukv, w_o_attn=w_o_attn,
                  conv_w=conv_w, conv_b=conv_b, dt_bias_f=dt_bias_f, dt_bias_b=dt_bias_b, a_log_f=a_log_f,
                  a_log_b=a_log_b, d_skip=d_skip, ssd_norm_w=ssd_norm_w, w_o_ssd=w_o_ssd, w_out=w_out,
                  ln1_g=ln1_g, ln1_b=ln1_b, w_router=w_router, router_bias=router_bias, w_gate=w_gate,
                  w_up=w_up, w_down=w_down, ws_gate=ws_gate, ws_up=ws_up, ws_down=ws_down, ln2_g=ln2_g,
                  ln2_b=ln2_b)
    assert w_in.shape[0] == DEPTH == 1
    params = {k: v[0] for k, v in params.items()}
    f = _prepare(params, (x_prompt.shape[1], x_sample.shape[1]))
    return (_layer(x_prompt, f), _layer(x_sample, f))
```

```python
import functools
import math

import jax
import jax.numpy as jnp
import numpy as np
from jax import lax
from jax.experimental import pallas as pl
from jax.experimental.pallas import tpu as pltpu
from jax.experimental.pallas import tpu_sc as plsc

F32 = jnp.float32
BF16 = jnp.bfloat16

D_MODEL = 1024
DEPTH = 1
N_HEADS = 16
QK_NOPE = 64
QK_ROPE = 32
V_DIM = 64
Q_LORA = 384
KV_LORA = 256
ROPE_THETA = 10000.0
D_INNER = 2 * D_MODEL
SSD_HEADDIM = 64
SSD_HEADS = D_INNER // SSD_HEADDIM
SSD_GROUPS = 4
D_STATE = 128
CONV_WIDTH = 5
CONV_DIM = D_INNER + 2 * SSD_GROUPS * D_STATE
CHUNK = 128
N_EXPERTS = 64
TOP_K = 8
N_EXPERT_GROUPS = 8
TOPK_GROUPS = 4
D_EXPERT = D_MODEL // 4
D_SHARED = D_MODEL // 4
ROUTED_SCALE = 2.5
LN_EPS = 1e-5
RMS_EPS = 1e-6
DN_ALPHA = (2 * DEPTH) ** 0.25

LANES = 128
HEAD_PAD = 128
QKV_W = 768
KROPE_OFF = Q_LORA + KV_LORA
EXPERT_BLOCK = 512
PACKED_W = D_MODEL // 2
VMEM_LIMIT = 56 * 1024 * 1024


def _cparams(sem, vmem=VMEM_LIMIT):
    return pltpu.CompilerParams(dimension_semantics=sem, vmem_limit_bytes=vmem)


def _dot(a, b):
    return jnp.dot(a, b, preferred_element_type=F32)


def _dot_nt(a, b):
    return lax.dot_general(a, b, (((1,), (1,)), ((), ())), preferred_element_type=F32)


def _split3(a):
    h = a.astype(BF16)
    r = a - h.astype(F32)
    m = r.astype(BF16)
    l = (r - m.astype(F32)).astype(BF16)
    return h, m, l


def _dot_split_lhs(a_f32, b_bf16):
    h, m, l = _split3(a_f32)
    return _dot(h, b_bf16) + _dot(m, b_bf16) + _dot(l, b_bf16)


def _silu(x):
    return x * jax.nn.sigmoid(x)


def _pack_rows(x):
    n = x.shape[1] // 2
    hi = lax.bitcast_convert_type(x[:, :n].astype(BF16).astype(F32), jnp.uint32)
    lo = lax.bitcast_convert_type(x[:, n:].astype(BF16).astype(F32), jnp.uint32)
    return hi | (lo >> 16)


def _unpack_rows(w):
    hi = lax.bitcast_convert_type(w & jnp.uint32(0xFFFF0000), F32)
    lo = lax.bitcast_convert_type(w << 16, F32)
    return hi, lo


def _layer_norm_rows(r, g, b):
    mu = jnp.mean(r, axis=-1, keepdims=True)
    d = r - mu
    var = jnp.mean(d * d, axis=-1, keepdims=True)
    return d * lax.rsqrt(var + LN_EPS) * g + b


def _inproj_body(x_ref, wq_ref, wz_ref, wx_ref, wdt_ref, wg_ref,
                 qkv_ref, z_ref, xbc_ref, dt_ref, g_ref):
    xb = x_ref[...].astype(BF16)
    qkv_ref[...] = _dot(xb, wq_ref[...]).astype(BF16)
    z_ref[...] = _silu(_dot(xb, wz_ref[...])).astype(BF16)
    xbc_ref[...] = _dot(xb, wx_ref[...]).astype(BF16)
    dt_ref[...] = _dot(xb, wdt_ref[...])
    g_ref[...] = jax.nn.sigmoid(_dot(xb, wg_ref[...])).astype(BF16)


def _in_proj(x2d, wq, wz, wx, wdt, wg, tm=512):
    t = x2d.shape[0]
    const = lambda w: pl.BlockSpec(w.shape, lambda i: (0, 0), pipeline_mode=pl.Buffered(1))
    row = lambda n: pl.BlockSpec((tm, n), lambda i: (i, 0))
    return pl.pallas_call(
        _inproj_body,
        grid=(t // tm,),
        in_specs=[row(D_MODEL), const(wq), const(wz), const(wx), const(wdt), const(wg)],
        out_specs=[row(QKV_W), row(D_INNER), row(CONV_DIM), row(LANES), row(2 * D_MODEL)],
        out_shape=[jax.ShapeDtypeStruct((t, QKV_W), BF16),
                   jax.ShapeDtypeStruct((t, D_INNER), BF16),
                   jax.ShapeDtypeStruct((t, CONV_DIM), BF16),
                   jax.ShapeDtypeStruct((t, LANES), F32),
                   jax.ShapeDtypeStruct((t, 2 * D_MODEL), BF16)],
        compiler_params=_cparams(("parallel",)),
        name="in_proj",
    )(x2d, wq, wz, wx, wdt, wg)


def _mla_prep_body(qkv_ref, qn_ref, kvn_ref, wq_ref, wk_ref, wv_ref, vone_ref, cq_ref, sq_ref, tk_ref,
                   q_ref, kt_ref, v_ref):
    qkv = qkv_ref[0].astype(F32)
    cq = qkv[:, :Q_LORA]
    cq = cq * lax.rsqrt(jnp.mean(cq * cq, axis=-1, keepdims=True) + RMS_EPS) * qn_ref[...]
    ckv = qkv[:, Q_LORA:KROPE_OFF]
    ckv = ckv * lax.rsqrt(jnp.mean(ckv * ckv, axis=-1, keepdims=True) + RMS_EPS) * kvn_ref[...]
    ckv_b = ckv.astype(BF16)
    qq = _dot(cq.astype(BF16), wq_ref[...])
    half = N_HEADS * HEAD_PAD
    cq_t = cq_ref[...]
    sq_t = sq_ref[...]
    for h in range(N_HEADS):
        lo = h * HEAD_PAD
        q_ref[0, :, lo:lo + HEAD_PAD] = (qq[:, lo:lo + HEAD_PAD] * cq_t
                                          + qq[:, half + lo:half + lo + HEAD_PAD] * sq_t).astype(BF16)
    kr = (qkv[:, KROPE_OFF:] * tk_ref[...]).astype(BF16)
    e = jnp.concatenate([ckv_b, kr], axis=1)
    kt_ref[0] = _dot_nt(wk_ref[...], e).astype(BF16)
    v_ref[0] = (_dot(ckv_b, wv_ref[...]) + vone_ref[...]).astype(BF16)


def _mla_prep(qkv, qn, kvn, wq, wk, wv, vone, cq_t, sq_t, tk_t, ts=512):
    b, s, _ = qkv.shape
    const = lambda w: pl.BlockSpec(w.shape, lambda bi, i: (0, 0))
    tab = pl.BlockSpec((ts, LANES), lambda bi, i: (i, 0))
    hw = N_HEADS * HEAD_PAD
    return pl.pallas_call(
        _mla_prep_body,
        grid=(b, s // ts),
        in_specs=[pl.BlockSpec((1, ts, QKV_W), lambda bi, i: (bi, i, 0)),
                  const(qn), const(kvn), const(wq), const(wk), const(wv), const(vone), tab, tab, tab],
        out_specs=[pl.BlockSpec((1, ts, hw), lambda bi, i: (bi, i, 0)),
                   pl.BlockSpec((1, hw, ts), lambda bi, i: (bi, 0, i)),
                   pl.BlockSpec((1, ts, hw), lambda bi, i: (bi, i, 0))],
        out_shape=[jax.ShapeDtypeStruct((b, s, hw), BF16),
                   jax.ShapeDtypeStruct((b, hw, s), BF16),
                   jax.ShapeDtypeStruct((b, s, hw), BF16)],
        compiler_params=_cparams(("parallel", "parallel")),
        name="mla_prep",
    )(qkv, qn, kvn, wq, wk, wv, vone, cq_t, sq_t, tk_t)


def _attn_body(q_ref, kt_ref, v_ref, o_ref):
    tq = q_ref.shape[1]
    for r in range(tq // ATTN_ROWS):
        rows = slice(r * ATTN_ROWS, (r + 1) * ATTN_ROWS)
        outs = []
        for hh in range(2):
            sl = slice(hh * HEAD_PAD, (hh + 1) * HEAD_PAD)
            s = _dot(q_ref[0, rows, sl], kt_ref[0, sl, :])
            m = jnp.max(s, axis=-1, keepdims=True)
            p = jnp.exp2(s - m).astype(BF16)
            o = _dot(p, v_ref[0, :, sl])
            outs.append(o / o[:, V_DIM:V_DIM + 1])
        lane = lax.broadcasted_iota(jnp.int32, outs[0].shape, 1)
        o_ref[0, rows, :] = jnp.where(lane < V_DIM, outs[0], pltpu.roll(outs[1], V_DIM, axis=1)).astype(BF16)


ATTN_ROWS = 256


def _attention(q, kt, v, tq=1024):
    b, s, _ = q.shape
    pairs = N_HEADS // 2
    return pl.pallas_call(
        _attn_body,
        grid=(b, pairs, s // tq),
        in_specs=[pl.BlockSpec((1, tq, 2 * HEAD_PAD), lambda bi, j, i: (bi, i, j)),
                  pl.BlockSpec((1, 2 * HEAD_PAD, s), lambda bi, j, i: (bi, j, 0)),
                  pl.BlockSpec((1, s, 2 * HEAD_PAD), lambda bi, j, i: (bi, 0, j))],
        out_specs=pl.BlockSpec((1, tq, 2 * V_DIM), lambda bi, j, i: (bi, i, j)),
        out_shape=jax.ShapeDtypeStruct((b, s, N_HEADS * V_DIM), BF16),
        compiler_params=_cparams(("parallel", "parallel", "arbitrary")),
        name="attention",
    )(q, kt, v)


CONV_HALO = 16


def _conv_body(prev_ref, main_ref, next_ref, w_ref, b_ref, o_ref, ext_ref):
    i = pl.program_id(1)
    n = pl.num_programs(1)
    ts = main_ref.shape[1]
    prev = prev_ref[0].astype(F32)
    nxt = next_ref[0].astype(F32)
    ext_ref[0:CONV_HALO, :] = jnp.where(i > 0, prev, 0.0)
    ext_ref[CONV_HALO:CONV_HALO + ts, :] = main_ref[0].astype(F32)
    ext_ref[CONV_HALO + ts:, :] = jnp.where(i < n - 1, nxt, 0.0)
    acc = b_ref[...]
    for k in range(CONV_WIDTH):
        off = CONV_HALO - CONV_WIDTH // 2 + k
        acc = acc + ext_ref[off:off + ts, :] * w_ref[k:k + 1, :]
    o_ref[0] = _silu(acc).astype(BF16)


def _conv(xbc, w8, bias, ts=512, tc=512):
    b, s, c = xbc.shape
    r = ts // CONV_HALO
    last = s // CONV_HALO - 1
    return pl.pallas_call(
        _conv_body,
        grid=(b, s // ts, c // tc),
        in_specs=[pl.BlockSpec((1, CONV_HALO, tc), lambda bi, i, j: (bi, jnp.maximum(i * r - 1, 0), j)),
                  pl.BlockSpec((1, ts, tc), lambda bi, i, j: (bi, i, j)),
                  pl.BlockSpec((1, CONV_HALO, tc), lambda bi, i, j: (bi, jnp.minimum((i + 1) * r, last), j)),
                  pl.BlockSpec((8, tc), lambda bi, i, j: (0, j)),
                  pl.BlockSpec((1, tc), lambda bi, i, j: (0, j))],
        out_specs=pl.BlockSpec((1, ts, tc), lambda bi, i, j: (bi, i, j)),
        out_shape=jax.ShapeDtypeStruct((b, s, c), BF16),
        scratch_shapes=[pltpu.VMEM((ts + 2 * CONV_HALO, tc), F32)],
        compiler_params=_cparams(("parallel", "parallel", "parallel")),
        name="conv",
    )(xbc, xbc, xbc, w8, bias)


def _ssd_body(reverse, xs_ref, b_ref, c_ref, dt_ref, dtb_ref, alog_ref, e_ref, y_ref, state_ref):
    @pl.when(pl.program_id(1) == 0)
    def _():
        state_ref[...] = jnp.zeros_like(state_ref)

    q = CHUNK
    lane0 = SSD_HEADS if reverse else 0
    rows = lax.broadcasted_iota(jnp.int32, (q, q), 0)
    cols = lax.broadcasted_iota(jnp.int32, (q, q), 1)
    mask = (rows <= cols) if reverse else (rows >= cols)
    tri = mask.astype(BF16)

    x = dt_ref[0] + dtb_ref[...]
    dt = jnp.maximum(x, 0.0) + jnp.log(1.0 + jnp.exp(-jnp.abs(x)))
    da = dt * (-jnp.exp(alog_ref[...]))
    cum = _dot_split_lhs_rhs(tri, da)
    cum_t = cum.T
    dt_t = dt.T
    total = cum[0:1, :] if reverse else cum[q - 1:q, :]
    w = dt * jnp.exp(total - cum)
    w_exp = _dot(w.astype(BF16), e_ref[...])
    xs = xs_ref[0]
    x_dec = (xs.astype(F32) * w_exp).astype(BF16)

    bm = b_ref[0].astype(F32)
    cm = c_ref[0]
    cm_f = cm.astype(F32)
    state_old = state_ref[...].astype(BF16)
    gw = D_STATE
    hp = SSD_HEADS // SSD_GROUPS * SSD_HEADDIM
    new_states = []
    for g in range(SSD_GROUPS):
        bt = bm[:, g * gw:(g + 1) * gw].T.astype(BF16)
        new_states.append(_dot(bt, x_dec[:, g * hp:(g + 1) * hp]))
        cb = _dot(cm[:, g * gw:(g + 1) * gw], bt)
        c_g = cm_f[:, g * gw:(g + 1) * gw]
        for jj in range(hp // LANES):
            j = g * (hp // LANES) + jj
            sl = slice(j * LANES, (j + 1) * LANES)
            rhs = jnp.concatenate([xs[:, sl], state_old[:, sl]], axis=0)
            ys = []
            for hh in range(2):
                ln = lane0 + 2 * j + hh
                col = jnp.broadcast_to(cum[:, ln:ln + 1], (q, q))
                seg = col - cum_t[ln:ln + 1, :]
                decay = jnp.exp(jnp.where(mask, seg, -jnp.inf))
                m = cb * decay * dt_t[ln:ln + 1, :]
                cs = c_g * jnp.exp(col)
                lhs = jnp.concatenate([m.astype(BF16), cs.astype(BF16)], axis=1)
                ys.append(_dot(lhs, rhs))
            lane = lax.broadcasted_iota(jnp.int32, (q, LANES), 1)
            y_ref[0, :, sl] = jnp.where(lane < SSD_HEADDIM, ys[0], ys[1]).astype(BF16)

    dec = jnp.broadcast_to(jnp.exp(total), (8, LANES))
    dec_exp = _dot_split_lhs(dec, e_ref[...])[0:1, :]
    state_ref[...] = state_ref[...] * dec_exp + jnp.concatenate(new_states, axis=1)


def _dot_split_lhs_rhs(tri_bf16, da_f32):
    h, m, l = _split3(da_f32)
    return _dot(tri_bf16, h) + _dot(tri_bf16, m) + _dot(tri_bf16, l)


def _ssd(conv_out, dt, dt_bias, a_log, expand, reverse):
    b, s, _ = conv_out.shape
    nc = s // CHUNK
    cidx = (lambda c: nc - 1 - c) if reverse else (lambda c: c)
    bcol = D_INNER // (SSD_GROUPS * D_STATE)
    gn = SSD_GROUPS * D_STATE
    const = lambda w: pl.BlockSpec(w.shape, lambda bi, c: (0, 0))
    return pl.pallas_call(
        functools.partial(_ssd_body, reverse),
        grid=(b, nc),
        in_specs=[pl.BlockSpec((1, CHUNK, D_INNER), lambda bi, c: (bi, cidx(c), 0)),
                  pl.BlockSpec((1, CHUNK, gn), lambda bi, c: (bi, cidx(c), bcol)),
                  pl.BlockSpec((1, CHUNK, gn), lambda bi, c: (bi, cidx(c), bcol + 1)),
                  pl.BlockSpec((1, CHUNK, LANES), lambda bi, c: (bi, cidx(c), 0)),
                  const(dt_bias), const(a_log), const(expand)],
        out_specs=pl.BlockSpec((1, CHUNK, D_INNER), lambda bi, c: (bi, cidx(c), 0)),
        out_shape=jax.ShapeDtypeStruct((b, s, D_INNER), BF16),
        scratch_shapes=[pltpu.VMEM((D_STATE, D_INNER), F32)],
        compiler_params=_cparams(("parallel", "arbitrary")),
        name="ssd_bwd" if reverse else "ssd_fwd",
    )(conv_out, conv_out, conv_out, dt, dt_bias, a_log, expand)


def _mix_body(x_ref, o_ref, yf_ref, yb_ref, xs_ref, z_ref, g_ref,
              woa_ref, wos_ref, wout_ref, dsk_ref, nw_ref, lg_ref, lb_ref, x1_ref, x1p_ref):
    attn = _dot(o_ref[...], woa_ref[...])
    y = yf_ref[...].astype(F32) + yb_ref[...].astype(F32) + dsk_ref[...] * xs_ref[...].astype(F32)
    y = y * z_ref[...].astype(F32)
    yn = y * lax.rsqrt(jnp.mean(y * y, axis=-1, keepdims=True) + RMS_EPS) * nw_ref[...]
    ssd = _dot(yn.astype(BF16), wos_ref[...])
    g = g_ref[...].astype(F32)
    mixed_in = g[:, :D_MODEL] * attn + g[:, D_MODEL:] * ssd
    mixed = _dot(mixed_in.astype(BF16), wout_ref[...])
    x1 = _layer_norm_rows(DN_ALPHA * x_ref[...] + mixed, lg_ref[...], lb_ref[...])
    x1_ref[...] = x1
    x1p_ref[...] = _pack_rows(x1)


def _mix(x2d, o, yf, yb, conv_out, z, g, woa, wos, wout, dsk, nw, lg, lb, tm=256):
    t = x2d.shape[0]
    const = lambda w: pl.BlockSpec(w.shape, lambda i: (0, 0))
    row = lambda n: pl.BlockSpec((tm, n), lambda i: (i, 0))
    return pl.pallas_call(
        _mix_body,
        grid=(t // tm,),
        in_specs=[row(D_MODEL), row(D_MODEL), row(D_INNER), row(D_INNER), row(D_INNER), row(D_INNER),
                  row(2 * D_MODEL), const(woa), const(wos), const(wout), const(dsk), const(nw),
                  const(lg), const(lb)],
        out_specs=[row(D_MODEL), row(PACKED_W)],
        out_shape=[jax.ShapeDtypeStruct((t, D_MODEL), F32), jax.ShapeDtypeStruct((t, PACKED_W), jnp.uint32)],
        compiler_params=_cparams(("parallel",)),
        name="mix_ln1",
    )(x2d, o, yf, yb, conv_out, z, g, woa, wos, wout, dsk, nw, lg, lb)


def _router_body(x_ref, wh_ref, wl_ref, bias_ref, su_ref, idx_ref, wts_ref, pos_ref, cnt_ref, carry_ref):
    @pl.when(pl.program_id(0) == 0)
    def _():
        carry_ref[...] = jnp.zeros_like(carry_ref)

    tm = x_ref.shape[0]
    x = x_ref[...]
    xh = x.astype(BF16)
    xl = (x - xh.astype(F32)).astype(BF16)
    wh = wh_ref[...]
    logits = _dot_nt(wh, xh) + _dot_nt(wh, xl) + _dot_nt(wl_ref[...], xh)
    scores = jax.nn.sigmoid(logits)
    choice = scores + bias_ref[...]
    gsz = N_EXPERTS // N_EXPERT_GROUPS
    shp = (N_EXPERT_GROUPS, gsz, tm)
    ch = choice.reshape(shp)
    sc = scores.reshape(shp)
    jio = lax.broadcasted_iota(jnp.int32, shp, 1)
    gio = lax.broadcasted_iota(jnp.int32, shp, 0)
    eio = gio * gsz + jio
    neg = -jnp.inf

    m1 = jnp.max(ch, axis=1, keepdims=True)
    i1 = jnp.min(jnp.where(ch == m1, jio, gsz), axis=1, keepdims=True)
    m2 = jnp.max(jnp.where(jio == i1, neg, ch), axis=1, keepdims=True)
    gs = m1 + m2
    gio1 = lax.broadcasted_iota(jnp.int32, (N_EXPERT_GROUPS, 1, tm), 0)
    sel = jnp.zeros(gs.shape, jnp.bool_)
    cur = gs
    for _ in range(TOPK_GROUPS):
        m = jnp.max(cur, axis=0, keepdims=True)
        gi = jnp.min(jnp.where(cur == m, gio1, N_EXPERT_GROUPS), axis=0, keepdims=True)
        hit = gio1 == gi
        sel = jnp.logical_or(sel, hit)
        cur = jnp.where(hit, neg, cur)
    masked = jnp.where(sel, ch, neg)

    def red(op, v):
        return op(op(v, axis=1, keepdims=True), axis=0, keepdims=True)

    hits, idxs, ws = [], [], []
    for _ in range(TOP_K):
        m = red(jnp.max, masked)
        ei = red(jnp.min, jnp.where(masked == m, eio, N_EXPERTS))
        hit = eio == ei
        hits.append(hit)
        idxs.append(ei)
        ws.append(red(jnp.sum, jnp.where(hit, sc, 0.0)))
        masked = jnp.where(hit, neg, masked)
    wsum = ws[0]
    for k in range(1, TOP_K):
        wsum = wsum + ws[k]

    onehot = hits[0]
    for k in range(1, TOP_K):
        onehot = jnp.logical_or(onehot, hits[k])
    oh = onehot.astype(F32).reshape(N_EXPERTS, tm)
    before = _dot(oh.astype(BF16), su_ref[...]) + carry_ref[...]
    before3 = before.reshape(shp)
    for k in range(TOP_K):
        idx_ref[k:k + 1, :] = idxs[k].reshape(1, tm)
        wts_ref[k:k + 1, :] = (ws[k] / wsum * ROUTED_SCALE).reshape(1, tm)
        pos_ref[k:k + 1, :] = red(jnp.sum, jnp.where(hits[k], before3, 0.0)).reshape(1, tm).astype(jnp.int32)
    carry_ref[...] = carry_ref[...] + jnp.sum(oh, axis=1, keepdims=True)
    cnt_ref[...] = jnp.broadcast_to(carry_ref[...], cnt_ref.shape)


def _router(x1, wr_hi, wr_lo, bias, su, tm=512):
    t = x1.shape[0]
    const = lambda w: pl.BlockSpec(w.shape, lambda i: (0, 0))
    out_tok = pl.BlockSpec((TOP_K, tm), lambda i: (0, i))
    return pl.pallas_call(
        _router_body,
        grid=(t // tm,),
        in_specs=[pl.BlockSpec((tm, D_MODEL), lambda i: (i, 0)), const(wr_hi), const(wr_lo), const(bias),
                  const(su)],
        out_specs=[out_tok, out_tok, out_tok, pl.BlockSpec((N_EXPERTS, LANES), lambda i: (0, 0))],
        out_shape=[jax.ShapeDtypeStruct((TOP_K, t), jnp.int32),
                   jax.ShapeDtypeStruct((TOP_K, t), F32),
                   jax.ShapeDtypeStruct((TOP_K, t), jnp.int32),
                   jax.ShapeDtypeStruct((N_EXPERTS, LANES), F32)],
        scratch_shapes=[pltpu.VMEM((N_EXPERTS, 1), F32)],
        compiler_params=_cparams(("arbitrary",)),
        name="router",
    )(x1, wr_hi, wr_lo, bias, su)


def _row_copy_wait(src_ref, dst_hbm, sem, n_rows, times):
    for _ in range(times):
        pltpu.make_async_copy(src_ref.at[pl.ds(0, n_rows)], dst_hbm.at[pl.ds(0, n_rows)], sem).wait()


def _scatter_body(ps_ref, pd_ref, nu_ref, dest_ref, x_ref, out_hbm, zero_ref, sem, zsem):
    tm = x_ref.shape[0]
    nb = out_hbm.shape[0] // EXPERT_BLOCK

    @pl.when(pl.program_id(0) == 0)
    def _():
        zero_ref[...] = jnp.zeros_like(zero_ref)

        def zcopy(row):
            start = pl.multiple_of(row, EXPERT_BLOCK)
            return pltpu.make_async_copy(zero_ref, out_hbm.at[pl.ds(start, EXPERT_BLOCK)], zsem)

        def last_block(e):
            return ps_ref[e] + pd_ref[e] - EXPERT_BLOCK

        def zstart(e, c):
            @pl.when(pd_ref[e] > 0)
            def _():
                zcopy(last_block(e)).start()
            return c

        def zwait(e, c):
            @pl.when(pd_ref[e] > 0)
            def _():
                zcopy(last_block(e)).wait()
            return c

        def tstart(b, c):
            zcopy(b * EXPERT_BLOCK).start()
            return c

        def twait(b, c):
            zcopy(b * EXPERT_BLOCK).wait()
            return c

        lax.fori_loop(0, N_EXPERTS, zstart, 0)
        lax.fori_loop(nu_ref[0], nb, tstart, 0)
        lax.fori_loop(0, N_EXPERTS, zwait, 0)
        lax.fori_loop(nu_ref[0], nb, twait, 0)

    def issue(t, c):
        for k in range(TOP_K):
            d = dest_ref[t * TOP_K + k]
            pltpu.make_async_copy(x_ref.at[pl.ds(t, 1)], out_hbm.at[pl.ds(d, 1)], sem).start()
        return c

    lax.fori_loop(0, tm, issue, 0)
    _row_copy_wait(x_ref, out_hbm, sem, tm, TOP_K)


def _scatter(pad_start, padded, n_used, dest_flat, x1, rows, tm=256):
    t = x1.shape[0]
    smem_tok = pl.BlockSpec((tm * TOP_K,), lambda i, ps, pd, nu: (i,), memory_space=pltpu.SMEM)
    grid_spec = pltpu.PrefetchScalarGridSpec(
        num_scalar_prefetch=3,
        grid=(t // tm,),
        in_specs=[smem_tok, pl.BlockSpec((tm, PACKED_W), lambda i, ps, pd, nu: (i, 0))],
        out_specs=pl.BlockSpec(memory_space=pl.ANY),
        scratch_shapes=[pltpu.VMEM((EXPERT_BLOCK, PACKED_W), jnp.uint32),
                        pltpu.SemaphoreType.DMA(()), pltpu.SemaphoreType.DMA(())],
    )
    return pl.pallas_call(
        _scatter_body,
        grid_spec=grid_spec,
        out_shape=jax.ShapeDtypeStruct((rows, PACKED_W), jnp.uint32),
        compiler_params=_cparams(("arbitrary",)),
        name="moe_scatter",
    )(pad_start, padded, n_used, dest_flat, x1)


def _expert_body(be_ref, x_ref, wg_ref, wu_ref, wd_ref, y_ref):
    hi, lo = _unpack_rows(x_ref[...])
    xb = jnp.concatenate([hi.astype(BF16), lo.astype(BF16)], axis=1)
    h = _silu(_dot(xb, wg_ref[0])) * _dot(xb, wu_ref[0])
    y_ref[...] = _pack_rows(_dot(h.astype(BF16), wd_ref[0]))


def _experts(block_e, xs, wg, wu, wd):
    rows = xs.shape[0]
    nb = rows // EXPERT_BLOCK
    grid_spec = pltpu.PrefetchScalarGridSpec(
        num_scalar_prefetch=1,
        grid=(nb,),
        in_specs=[pl.BlockSpec((EXPERT_BLOCK, PACKED_W), lambda b, be: (b, 0)),
                  pl.BlockSpec((1, D_MODEL, D_EXPERT), lambda b, be: (be[b], 0, 0)),
                  pl.BlockSpec((1, D_MODEL, D_EXPERT), lambda b, be: (be[b], 0, 0)),
                  pl.BlockSpec((1, D_EXPERT, D_MODEL), lambda b, be: (be[b], 0, 0))],
        out_specs=pl.BlockSpec((EXPERT_BLOCK, PACKED_W), lambda b, be: (b, 0)),
    )
    return pl.pallas_call(
        _expert_body,
        grid_spec=grid_spec,
        out_shape=jax.ShapeDtypeStruct((rows, PACKED_W), jnp.uint32),
        compiler_params=_cparams(("parallel",)),
        name="moe_experts",
    )(block_e, xs, wg, wu, wd)


def _final_body(x1_ref, wt_ref, g_ref, wsg_ref, wsu_ref, wsd_ref, lg_ref, lb_ref, out_ref):
    x1 = x1_ref[...]
    xb = x1.astype(BF16)
    h = _silu(_dot(xb, wsg_ref[...])) * _dot(xb, wsu_ref[...])
    acc = DN_ALPHA * x1 + _dot(h.astype(BF16), wsd_ref[...])
    wt = wt_ref[...]
    acc_hi, acc_lo = acc[:, :PACKED_W], acc[:, PACKED_W:]
    for k in range(TOP_K):
        hi, lo = _unpack_rows(g_ref[k])
        acc_hi = acc_hi + wt[:, k:k + 1] * hi
        acc_lo = acc_lo + wt[:, k:k + 1] * lo
    acc = jnp.concatenate([acc_hi, acc_lo], axis=1)
    out_ref[...] = _layer_norm_rows(acc, lg_ref[...], lb_ref[...])


def _final(x1, wt_rows, g, wsg, wsu, wsd, lg, lb, tm=256):
    t = x1.shape[0]
    const = lambda w: pl.BlockSpec(w.shape, lambda i: (0, 0))
    return pl.pallas_call(
        _final_body,
        grid=(t // tm,),
        in_specs=[pl.BlockSpec((tm, D_MODEL), lambda i: (i, 0)),
                  pl.BlockSpec((tm, TOP_K), lambda i: (i, 0)),
                  pl.BlockSpec((TOP_K, tm, PACKED_W), lambda i: (0, i, 0)),
                  const(wsg), const(wsu), const(wsd), const(lg), const(lb)],
        out_specs=pl.BlockSpec((tm, D_MODEL), lambda i: (i, 0)),
        out_shape=jax.ShapeDtypeStruct((t, D_MODEL), F32),
        compiler_params=_cparams(("parallel",)),
        name="moe_combine_ln2",
    )(x1, wt_rows, g, wsg, wsu, wsd, lg, lb)


SC_CORES = 2
SC_SUBCORES = 16
SC_CHUNK = 128


def _sc_gather_rows(table, idx):
    n = idx.shape[0]
    w = table.shape[1]
    workers = SC_CORES * SC_SUBCORES
    per_w = n // workers
    steps = per_w // SC_CHUNK
    assert steps * SC_CHUNK * workers == n
    mesh = plsc.VectorSubcoreMesh(core_axis_name="c", subcore_axis_name="s")

    def body(table_hbm, idx_hbm, out_hbm, idx_v, rows_v, sem):
        wid = lax.axis_index("s") * SC_CORES + lax.axis_index("c")
        pltpu.sync_copy(idx_hbm.at[wid], idx_v)

        @pl.loop(0, steps)
        def _(i):
            off = pl.multiple_of(wid * per_w + i * SC_CHUNK, SC_CHUNK)
            pltpu.async_copy(table_hbm.at[idx_v.at[i]], rows_v, sem).wait()
            pltpu.sync_copy(rows_v, out_hbm.at[pl.ds(off, SC_CHUNK)])

    return pl.kernel(
        body, mesh=mesh,
        out_type=jax.ShapeDtypeStruct((n, w), table.dtype),
        scratch_types=[pltpu.VMEM((steps, SC_CHUNK), jnp.int32), pltpu.VMEM((SC_CHUNK, w), table.dtype),
                       pltpu.SemaphoreType.DMA],
    )(table, idx.reshape(workers, steps, SC_CHUNK))


def _dest_body(ps_ref, idx_ref, pos_ref, out_ref):
    idx = idx_ref[...]
    acc = pos_ref[...]
    for e in range(N_EXPERTS):
        acc = acc + jnp.where(idx == e, ps_ref[e], 0)
    out_ref[...] = acc


def _dest_rows(pad_start, idx, pos, tn=2048):
    t = idx.shape[1]
    tn = min(tn, t)
    tok = pl.BlockSpec((TOP_K, tn), lambda i, ps: (0, i))
    grid_spec = pltpu.PrefetchScalarGridSpec(
        num_scalar_prefetch=1, grid=(t // tn,), in_specs=[tok, tok], out_specs=tok)
    return pl.pallas_call(
        _dest_body,
        grid_spec=grid_spec,
        out_shape=jax.ShapeDtypeStruct((TOP_K, t), jnp.int32),
        compiler_params=_cparams(("parallel",)),
        name="moe_dest",
    )(pad_start, idx, pos)


def _rot_cols(w):
    half = w.shape[-1] // 2
    return jnp.concatenate([-w[..., half:], w[..., :half]], axis=-1)


def _prepare(p, seq_lens):
    f = {}
    w_in = p["w_in"]
    o = np.cumsum((0, Q_LORA, KV_LORA, QK_ROPE, D_INNER, CONV_DIM, SSD_HEADS, SSD_HEADS, D_MODEL, D_MODEL))
    seg = lambda i: w_in[:, o[i]:o[i + 1]]
    zeros = lambda n: jnp.zeros((D_MODEL, n), F32)
    f["wq"] = jnp.concatenate([seg(0), seg(1), seg(2), _rot_cols(seg(2)), zeros(QKV_W - KROPE_OFF - 2 * QK_ROPE)],
                              axis=1).astype(BF16)
    f["wz"] = seg(3).astype(BF16)
    f["wx"] = seg(4).astype(BF16)
    f["wdt"] = jnp.concatenate([seg(5), seg(6), zeros(LANES - 2 * SSD_HEADS)], axis=1).astype(BF16)
    f["wg"] = jnp.concatenate([seg(7), seg(8)], axis=1).astype(BF16)

    f["qn"] = p["q_norm_w"].reshape(1, Q_LORA)
    f["kvn"] = p["kv_norm_w"].reshape(1, KV_LORA)
    w_uq = p["w_uq"]
    nope, rope = w_uq[..., :QK_NOPE], w_uq[..., QK_NOPE:]
    padq = HEAD_PAD - QK_NOPE - QK_ROPE
    zq = lambda n: jnp.zeros((Q_LORA, N_HEADS, n), F32)
    wq_a = jnp.concatenate([nope, rope, zq(padq)], axis=-1).reshape(Q_LORA, N_HEADS * HEAD_PAD)
    wq_b = jnp.concatenate([zq(QK_NOPE), _rot_cols(rope), zq(padq)], axis=-1).reshape(Q_LORA, N_HEADS * HEAD_PAD)
    f["w_q"] = jnp.concatenate([wq_a, wq_b], axis=1).astype(BF16)
    w_ukv = p["w_ukv"]
    wk_t = jnp.transpose(w_ukv[..., :QK_NOPE], (1, 2, 0))
    eye = jnp.eye(QK_ROPE, dtype=F32)
    rope_rows = jnp.concatenate([jnp.zeros((QK_ROPE, KV_LORA), F32), eye, eye,
                                 jnp.zeros((QK_ROPE, LANES - 2 * QK_ROPE), F32)], axis=1)
    wk = jnp.concatenate([
        jnp.concatenate([wk_t, jnp.zeros((N_HEADS, QK_NOPE, LANES), F32)], axis=2),
        jnp.broadcast_to(rope_rows, (N_HEADS, QK_ROPE, KV_LORA + LANES)),
        jnp.zeros((N_HEADS, padq, KV_LORA + LANES), F32)], axis=1)
    f["w_k"] = wk.reshape(N_HEADS * HEAD_PAD, KV_LORA + LANES).astype(BF16)
    w_v = jnp.concatenate([w_ukv[..., QK_NOPE:], jnp.zeros((KV_LORA, N_HEADS, HEAD_PAD - V_DIM), F32)], axis=-1)
    f["w_v"] = w_v.reshape(KV_LORA, N_HEADS * HEAD_PAD).astype(BF16)
    f["v_one"] = (jnp.arange(N_HEADS * HEAD_PAD) % HEAD_PAD == V_DIM).astype(F32).reshape(1, -1)

    scale = (QK_NOPE + QK_ROPE) ** -0.5 * math.log2(math.e)
    f["rope"] = {}
    for s in sorted(set(seq_lens)):
        inv_freq = ROPE_THETA ** (-jnp.arange(0, QK_ROPE, 2, dtype=F32) / QK_ROPE)
        ang = jnp.arange(s, dtype=F32)[:, None] * inv_freq[None, :]
        cos, sin = jnp.cos(ang), jnp.sin(ang)
        one, zero = jnp.ones((s, QK_NOPE), F32), jnp.zeros((s, QK_NOPE), F32)
        zp = jnp.zeros((s, padq), F32)
        cq_t = scale * jnp.concatenate([one, cos, cos, zp], axis=1)
        sq_t = scale * jnp.concatenate([zero, sin, sin, zp], axis=1)
        tk_t = jnp.concatenate([cos, cos, sin, sin, jnp.zeros((s, LANES - 2 * QK_ROPE), F32)], axis=1)
        f["rope"][s] = (cq_t, sq_t, tk_t)

    f["conv_w"] = jnp.concatenate([p["conv_w"], jnp.zeros((8 - CONV_WIDTH, CONV_DIM), F32)], axis=0)
    f["conv_b"] = p["conv_b"].reshape(1, CONV_DIM)
    pad_h = jnp.zeros((LANES - 2 * SSD_HEADS,), F32)
    f["dt_bias"] = jnp.concatenate([p["dt_bias_f"], p["dt_bias_b"], pad_h]).reshape(1, LANES)
    f["a_log"] = jnp.concatenate([p["a_log_f"], p["a_log_b"], pad_h]).reshape(1, LANES)
    head_of_lane = jnp.arange(D_INNER, dtype=jnp.int32) // SSD_HEADDIM
    lane_id = jnp.arange(LANES, dtype=jnp.int32)[:, None]
    f["expand_f"] = (lane_id == head_of_lane[None, :]).astype(BF16)
    f["expand_b"] = (lane_id == head_of_lane[None, :] + SSD_HEADS).astype(BF16)
    f["d_skip"] = jnp.repeat(p["d_skip"], SSD_HEADDIM).reshape(1, D_INNER)
    f["ssd_nw"] = p["ssd_norm_w"].reshape(1, D_INNER)
    f["w_oa"] = p["w_o_attn"].astype(BF16)
    f["w_os"] = p["w_o_ssd"].astype(BF16)
    f["w_out"] = p["w_out"].astype(BF16)
    f["ln1_g"] = p["ln1_g"].reshape(1, D_MODEL)
    f["ln1_b"] = p["ln1_b"].reshape(1, D_MODEL)

    wr_t = p["w_router"].T
    wr_hi = wr_t.astype(BF16)
    f["wr_hi"] = wr_hi
    f["wr_lo"] = (wr_t - wr_hi.astype(F32)).astype(BF16)
    f["r_bias"] = p["router_bias"].reshape(N_EXPERTS, 1)
    f["w_gate"] = p["w_gate"].astype(BF16)
    f["w_up"] = p["w_up"].astype(BF16)
    f["w_down"] = p["w_down"].astype(BF16)
    f["ws_gate"] = p["ws_gate"].astype(BF16)
    f["ws_up"] = p["ws_up"].astype(BF16)
    f["ws_down"] = p["ws_down"].astype(BF16)
    f["ln2_g"] = p["ln2_g"].reshape(1, D_MODEL)
    f["ln2_b"] = p["ln2_b"].reshape(1, D_MODEL)
    return f


ROUTER_TILE = 512


def _layer(x, f):
    b, s, d = x.shape
    t = b * s
    x2d = x.reshape(t, d)
    qkv, z, xbc, dt, g = _in_proj(x2d, f["wq"], f["wz"], f["wx"], f["wdt"], f["wg"])

    cq_t, sq_t, tk_t = f["rope"][s]
    q, kt, v = _mla_prep(qkv.reshape(b, s, QKV_W), f["qn"], f["kvn"], f["w_q"], f["w_k"], f["w_v"], f["v_one"],
                         cq_t, sq_t, tk_t)
    o = _attention(q, kt, v)

    conv_out = _conv(xbc.reshape(b, s, CONV_DIM), f["conv_w"], f["conv_b"])
    dt3 = dt.reshape(b, s, LANES)
    y_f = _ssd(conv_out, dt3, f["dt_bias"], f["a_log"], f["expand_f"], reverse=False)
    y_b = _ssd(conv_out, dt3, f["dt_bias"], f["a_log"], f["expand_b"], reverse=True)

    x1, x1p = _mix(x2d, o.reshape(t, -1), y_f.reshape(t, -1), y_b.reshape(t, -1), conv_out.reshape(t, -1), z, g,
              f["w_oa"], f["w_os"], f["w_out"], f["d_skip"], f["ssd_nw"], f["ln1_g"], f["ln1_b"])

    su = (jnp.arange(ROUTER_TILE)[:, None] < jnp.arange(ROUTER_TILE)[None, :]).astype(BF16)
    idx, wts, pos, cnt = _router(x1, f["wr_hi"], f["wr_lo"], f["r_bias"], su, tm=ROUTER_TILE)

    counts = cnt[:, 0].astype(jnp.int32)
    padded = (counts + EXPERT_BLOCK - 1) // EXPERT_BLOCK * EXPERT_BLOCK
    pad_end = jnp.cumsum(padded)
    pad_start = pad_end - padded
    nb = t * TOP_K // EXPERT_BLOCK + N_EXPERTS
    n_used = pad_end[-1] // EXPERT_BLOCK
    blk_row = jnp.minimum(jnp.arange(nb, dtype=jnp.int32), n_used - 1) * EXPERT_BLOCK
    block_e = jnp.minimum(jnp.sum((blk_row[:, None] >= pad_end[None, :]).astype(jnp.int32), axis=1), N_EXPERTS - 1)

    dest = _dest_rows(pad_start, idx, pos)
    dest_flat = dest.T.reshape(t * TOP_K)
    xs = _scatter(pad_start, padded, n_used.reshape(1), dest_flat, x1p, nb * EXPERT_BLOCK)
    ys = _experts(block_e, xs, f["w_gate"], f["w_up"], f["w_down"])
    g = _sc_gather_rows(ys, dest.reshape(TOP_K * t)).reshape(TOP_K, t, PACKED_W)
    y = _final(x1, wts.T, g, f["ws_gate"], f["ws_up"], f["ws_down"], f["ln2_g"], f["ln2_b"])
    return y.reshape(b, s, d)


def kernel(x_prompt, x_sample, w_in, q_norm_w, kv_norm_w, w_uq, w_ukv, w_o_attn, conv_w, conv_b, dt_bias_f,
           dt_bias_b, a_log_f, a_log_b, d_skip, ssd_norm_w, w_o_ssd, w_out, ln1_g, ln1_b, w_router, router_bias,
           w_gate, w_up, w_down, ws_gate, ws_up, ws_down, ln2_g, ln2_b):
    params = dict(w_in=w_in, q_norm_w=q_norm_w, kv_norm_w=kv_norm_w, w_uq=w_uq, w_ukv=w_ukv, w_o_attn=w_o_attn,
                  conv_w=conv_w, conv_b=conv_b, dt_bias_f=dt_bias_f, dt_bias_b=dt_bias_b, a_log_f=a_log_f,
                  a_log_b=a_log_b, d_skip=d_skip, ssd_norm_w=ssd_norm_w, w_o_ssd=w_o_ssd, w_out=w_out,
                  ln1_g=ln1_g, ln1_b=ln1_b, w_router=w_router, router_bias=router_bias, w_gate=w_gate,
                  w_up=w_up, w_down=w_down, ws_gate=ws_gate, ws_up=ws_up, ws_down=ws_down, ln2_g=ln2_g,
                  ln2_b=ln2_b)
    assert w_in.shape[0] == DEPTH == 1
    params = {k: v[0] for k, v in params.items()}
    f = _prepare(params, (x_prompt.shape[1], x_sample.shape[1]))
    return (_layer(x_prompt, f), _layer(x_sample, f))
```

```python
import functools
import math

import jax
import jax.numpy as jnp
import numpy as np
from jax import lax
from jax.experimental import pallas as pl
from jax.experimental.pallas import tpu as pltpu
from jax.experimental.pallas import tpu_sc as plsc

F32 = jnp.float32
BF16 = jnp.bfloat16

D_MODEL = 1024
DEPTH = 1
N_HEADS = 16
QK_NOPE = 64
QK_ROPE = 32
V_DIM = 64
Q_LORA = 384
KV_LORA = 256
ROPE_THETA = 10000.0
D_INNER = 2 * D_MODEL
SSD_HEADDIM = 64
SSD_HEADS = D_INNER // SSD_HEADDIM
SSD_GROUPS = 4
D_STATE = 128
CONV_WIDTH = 5
CONV_DIM = D_INNER + 2 * SSD_GROUPS * D_STATE
CHUNK = 128
N_EXPERTS = 64
TOP_K = 8
N_EXPERT_GROUPS = 8
TOPK_GROUPS = 4
D_EXPERT = D_MODEL // 4
D_SHARED = D_MODEL // 4
ROUTED_SCALE = 2.5
LN_EPS = 1e-5
RMS_EPS = 1e-6
DN_ALPHA = (2 * DEPTH) ** 0.25

LANES = 128
HEAD_PAD = 128
QKV_W = 768
KROPE_OFF = Q_LORA + KV_LORA
EXPERT_BLOCK = 512
PACKED_W = D_MODEL // 2
VMEM_LIMIT = 56 * 1024 * 1024


def _cparams(sem, vmem=VMEM_LIMIT):
    return pltpu.CompilerParams(dimension_semantics=sem, vmem_limit_bytes=vmem)


def _dot(a, b):
    return jnp.dot(a, b, preferred_element_type=F32)


def _dot_nt(a, b):
    return lax.dot_general(a, b, (((1,), (1,)), ((), ())), preferred_element_type=F32)


def _split3(a):
    h = a.astype(BF16)
    r = a - h.astype(F32)
    m = r.astype(BF16)
    l = (r - m.astype(F32)).astype(BF16)
    return h, m, l


def _dot_split_lhs(a_f32, b_bf16):
    h, m, l = _split3(a_f32)
    return _dot(h, b_bf16) + _dot(m, b_bf16) + _dot(l, b_bf16)


def _silu(x):
    return x * jax.nn.sigmoid(x)


def _pack_rows(x):
    n = x.shape[1] // 2
    hi = lax.bitcast_convert_type(x[:, :n].astype(BF16).astype(F32), jnp.uint32)
    lo = lax.bitcast_convert_type(x[:, n:].astype(BF16).astype(F32), jnp.uint32)
    return hi | (lo >> 16)


def _unpack_rows(w):
    hi = lax.bitcast_convert_type(w & jnp.uint32(0xFFFF0000), F32)
    lo = lax.bitcast_convert_type(w << 16, F32)
    return hi, lo


def _layer_norm_rows(r, g, b):
    mu = jnp.mean(r, axis=-1, keepdims=True)
    d = r - mu
    var = jnp.mean(d * d, axis=-1, keepdims=True)
    return d * lax.rsqrt(var + LN_EPS) * g + b


def _inproj_body(x_ref, wq_ref, wz_ref, wx_ref, wdt_ref, wg_ref,
                 qkv_ref, z_ref, xbc_ref, dt_ref, g_ref):
    xb = x_ref[...].astype(BF16)
    qkv_ref[...] = _dot(xb, wq_ref[...]).astype(BF16)
    z_ref[...] = _silu(_dot(xb, wz_ref[...])).astype(BF16)
    xbc_ref[...] = _dot(xb, wx_ref[...]).astype(BF16)
    dt_ref[...] = _dot(xb, wdt_ref[...])
    g_ref[...] = jax.nn.sigmoid(_dot(xb, wg_ref[...])).astype(BF16)


def _in_proj(x2d, wq, wz, wx, wdt, wg, tm=512):
    t = x2d.shape[0]
    const = lambda w: pl.BlockSpec(w.shape, lambda i: (0, 0), pipeline_mode=pl.Buffered(1))
    row = lambda n: pl.BlockSpec((tm, n), lambda i: (i, 0))
    return pl.pallas_call(
        _inproj_body,
        grid=(t // tm,),
        in_specs=[row(D_MODEL), const(wq), const(wz), const(wx), const(wdt), const(wg)],
        out_specs=[row(QKV_W), row(D_INNER), row(CONV_DIM), row(LANES), row(2 * D_MODEL)],
        out_shape=[jax.ShapeDtypeStruct((t, QKV_W), BF16),
                   jax.ShapeDtypeStruct((t, D_INNER), BF16),
                   jax.ShapeDtypeStruct((t, CONV_DIM), BF16),
                   jax.ShapeDtypeStruct((t, LANES), F32),
                   jax.ShapeDtypeStruct((t, 2 * D_MODEL), BF16)],
        compiler_params=_cparams(("parallel",)),
        name="in_proj",
    )(x2d, wq, wz, wx, wdt, wg)


def _mla_prep_body(qkv_ref, qn_ref, kvn_ref, wq_ref, wk_ref, wv_ref, vone_ref, cq_ref, sq_ref, tk_ref,
                   q_ref, kt_ref, v_ref):
    qkv = qkv_ref[0].astype(F32)
    cq = qkv[:, :Q_LORA]
    cq = cq * lax.rsqrt(jnp.mean(cq * cq, axis=-1, keepdims=True) + RMS_EPS) * qn_ref[...]
    ckv = qkv[:, Q_LORA:KROPE_OFF]
    ckv = ckv * lax.rsqrt(jnp.mean(ckv * ckv, axis=-1, keepdims=True) + RMS_EPS) * kvn_ref[...]
    ckv_b = ckv.astype(BF16)
    qq = _dot(cq.astype(BF16), wq_ref[...])
    half = N_HEADS * HEAD_PAD
    cq_t = cq_ref[...]
    sq_t = sq_ref[...]
    for h in range(N_HEADS):
        lo = h * HEAD_PAD
        q_ref[0, :, lo:lo + HEAD_PAD] = (qq[:, lo:lo + HEAD_PAD] * cq_t
                                          + qq[:, half + lo:half + lo + HEAD_PAD] * sq_t).astype(BF16)
    kr = (qkv[:, KROPE_OFF:] * tk_ref[...]).astype(BF16)
    e = jnp.concatenate([ckv_b, kr], axis=1)
    kt_ref[0] = _dot_nt(wk_ref[...], e).astype(BF16)
    v_ref[0] = (_dot(ckv_b, wv_ref[...]) + vone_ref[...]).astype(BF16)


def _mla_prep(qkv, qn, kvn, wq, wk, wv, vone, cq_t, sq_t, tk_t, ts=512):
    b, s, _ = qkv.shape
    const = lambda w: pl.BlockSpec(w.shape, lambda bi, i: (0, 0))
    tab = pl.BlockSpec((ts, LANES), lambda bi, i: (i, 0))
    hw = N_HEADS * HEAD_PAD
    return pl.pallas_call(
        _mla_prep_body,
        grid=(b, s // ts),
        in_specs=[pl.BlockSpec((1, ts, QKV_W), lambda bi, i: (bi, i, 0)),
                  const(qn), const(kvn), const(wq), const(wk), const(wv), const(vone), tab, tab, tab],
        out_specs=[pl.BlockSpec((1, ts, hw), lambda bi, i: (bi, i, 0)),
                   pl.BlockSpec((1, hw, ts), lambda bi, i: (bi, 0, i)),
                   pl.BlockSpec((1, ts, hw), lambda bi, i: (bi, i, 0))],
        out_shape=[jax.ShapeDtypeStruct((b, s, hw), BF16),
                   jax.ShapeDtypeStruct((b, hw, s), BF16),
                   jax.ShapeDtypeStruct((b, s, hw), BF16)],
        compiler_params=_cparams(("parallel", "parallel")),
        name="mla_prep",
    )(qkv, qn, kvn, wq, wk, wv, vone, cq_t, sq_t, tk_t)


def _attn_body(q_ref, kt_ref, v_ref, o_ref):
    tq = q_ref.shape[1]
    for r in range(tq // ATTN_ROWS):
        rows = slice(r * ATTN_ROWS, (r + 1) * ATTN_ROWS)
        outs = []
        for hh in range(2):
            sl = slice(hh * HEAD_PAD, (hh + 1) * HEAD_PAD)
            s = _dot(q_ref[0, rows, sl], kt_ref[0, sl, :])
            m = jnp.max(s, axis=-1, keepdims=True)
            p = jnp.exp2(s - m).astype(BF16)
            o = _dot(p, v_ref[0, :, sl])
            outs.append(o / o[:, V_DIM:V_DIM + 1])
        lane = lax.broadcasted_iota(jnp.int32, outs[0].shape, 1)
        o_ref[0, rows, :] = jnp.where(lane < V_DIM, outs[0], pltpu.roll(outs[1], V_DIM, axis=1)).astype(BF16)


ATTN_ROWS = 256


def _attention(q, kt, v, tq=1024):
    b, s, _ = q.shape
    pairs = N_HEADS // 2
    return pl.pallas_call(
        _attn_body,
        grid=(b, pairs, s // tq),
        in_specs=[pl.BlockSpec((1, tq, 2 * HEAD_PAD), lambda bi, j, i: (bi, i, j)),
                  pl.BlockSpec((1, 2 * HEAD_PAD, s), lambda bi, j, i: (bi, j, 0)),
                  pl.BlockSpec((1, s, 2 * HEAD_PAD), lambda bi, j, i: (bi, 0, j))],
        out_specs=pl.BlockSpec((1, tq, 2 * V_DIM), lambda bi, j, i: (bi, i, j)),
        out_shape=jax.ShapeDtypeStruct((b, s, N_HEADS * V_DIM), BF16),
        compiler_params=_cparams(("parallel", "parallel", "arbitrary")),
        name="attention",
    )(q, kt, v)


CONV_HALO = 16


def _conv_body(prev_ref, main_ref, next_ref, w_ref, b_ref, o_ref, ext_ref):
    i = pl.program_id(1)
    n = pl.num_programs(1)
    ts = main_ref.shape[1]
    prev = prev_ref[0].astype(F32)
    nxt = next_ref[0].astype(F32)
    ext_ref[0:CONV_HALO, :] = jnp.where(i > 0, prev, 0.0)
    ext_ref[CONV_HALO:CONV_HALO + ts, :] = main_ref[0].astype(F32)
    ext_ref[CONV_HALO + ts:, :] = jnp.where(i < n - 1, nxt, 0.0)
    acc = b_ref[...]
    for k in range(CONV_WIDTH):
        off = CONV_HALO - CONV_WIDTH // 2 + k
        acc = acc + ext_ref[off:off + ts, :] * w_ref[k:k + 1, :]
    o_ref[0] = _silu(acc).astype(BF16)


def _conv(xbc, w8, bias, ts=512, tc=512):
    b, s, c = xbc.shape
    r = ts // CONV_HALO
    last = s // CONV_HALO - 1
    return pl.pallas_call(
        _conv_body,
        grid=(b, s // ts, c // tc),
        in_specs=[pl.BlockSpec((1, CONV_HALO, tc), lambda bi, i, j: (bi, jnp.maximum(i * r - 1, 0), j)),
                  pl.BlockSpec((1, ts, tc), lambda bi, i, j: (bi, i, j)),
                  pl.BlockSpec((1, CONV_HALO, tc), lambda bi, i, j: (bi, jnp.minimum((i + 1) * r, last), j)),
                  pl.BlockSpec((8, tc), lambda bi, i, j: (0, j)),
                  pl.BlockSpec((1, tc), lambda bi, i, j: (0, j))],
        out_specs=pl.BlockSpec((1, ts, tc), lambda bi, i, j: (bi, i, j)),
        out_shape=jax.ShapeDtypeStruct((b, s, c), BF16),
        scratch_shapes=[pltpu.VMEM((ts + 2 * CONV_HALO, tc), F32)],
        compiler_params=_cparams(("parallel", "parallel", "parallel")),
        name="conv",
    )(xbc, xbc, xbc, w8, bias)


def _ssd_body(reverse, xs_ref, b_ref, c_ref, dt_ref, dtb_ref, alog_ref, e_ref, y_ref, state_ref):
    @pl.when(pl.program_id(1) == 0)
    def _():
        state_ref[...] = jnp.zeros_like(state_ref)

    q = CHUNK
    lane0 = SSD_HEADS if reverse else 0
    rows = lax.broadcasted_iota(jnp.int32, (q, q), 0)
    cols = lax.broadcasted_iota(jnp.int32, (q, q), 1)
    mask = (rows <= cols) if reverse else (rows >= cols)
    tri = mask.astype(BF16)

    x = dt_ref[0] + dtb_ref[...]
    dt = jnp.maximum(x, 0.0) + jnp.log(1.0 + jnp.exp(-jnp.abs(x)))
    da = dt * (-jnp.exp(alog_ref[...]))
    cum = _dot_split_lhs_rhs(tri, da)
    cum_t = cum.T
    dt_t = dt.T
    total = cum[0:1, :] if reverse else cum[q - 1:q, :]
    w = dt * jnp.exp(total - cum)
    w_exp = _dot(w.astype(BF16), e_ref[...])
    xs = xs_ref[0]
    x_dec = (xs.astype(F32) * w_exp).astype(BF16)

    bm = b_ref[0].astype(F32)
    cm = c_ref[0]
    cm_f = cm.astype(F32)
    state_old = state_ref[...].astype(BF16)
    gw = D_STATE
    hp = SSD_HEADS // SSD_GROUPS * SSD_HEADDIM
    new_states = []
    for g in range(SSD_GROUPS):
        bt = bm[:, g * gw:(g + 1) * gw].T.astype(BF16)
        new_states.append(_dot(bt, x_dec[:, g * hp:(g + 1) * hp]))
        cb = _dot(cm[:, g * gw:(g + 1) * gw], bt)
        c_g = cm_f[:, g * gw:(g + 1) * gw]
        for jj in range(hp // LANES):
            j = g * (hp // LANES) + jj
            sl = slice(j * LANES, (j + 1) * LANES)
            rhs = jnp.concatenate([xs[:, sl], state_old[:, sl]], axis=0)
            ys = []
            for hh in range(2):
                ln = lane0 + 2 * j + hh
                col = jnp.broadcast_to(cum[:, ln:ln + 1], (q, q))
                seg = col - cum_t[ln:ln + 1, :]
                decay = jnp.exp(jnp.where(mask, seg, -jnp.inf))
                m = cb * decay * dt_t[ln:ln + 1, :]
                cs = c_g * jnp.exp(col)
                lhs = jnp.concatenate([m.astype(BF16), cs.astype(BF16)], axis=1)
                ys.append(_dot(lhs, rhs))
            lane = lax.broadcasted_iota(jnp.int32, (q, LANES), 1)
            y_ref[0, :, sl] = jnp.where(lane < SSD_HEADDIM, ys[0], ys[1]).astype(BF16)

    dec = jnp.broadcast_to(jnp.exp(total), (8, LANES))
    dec_exp = _dot_split_lhs(dec, e_ref[...])[0:1, :]
    state_ref[...] = state_ref[...] * dec_exp + jnp.concatenate(new_states, axis=1)


def _dot_split_lhs_rhs(tri_bf16, da_f32):
    h, m, l = _split3(da_f32)
    return _dot(tri_bf16, h) + _dot(tri_bf16, m) + _dot(tri_bf16, l)


def _ssd(conv_out, dt, dt_bias, a_log, expand, reverse):
    b, s, _ = conv_out.shape
    nc = s // CHUNK
    cidx = (lambda c: nc - 1 - c) if reverse else (lambda c: c)
    bcol = D_INNER // (SSD_GROUPS * D_STATE)
    gn = SSD_GROUPS * D_STATE
    const = lambda w: pl.BlockSpec(w.shape, lambda bi, c: (0, 0))
    return pl.pallas_call(
        functools.partial(_ssd_body, reverse),
        grid=(b, nc),
        in_specs=[pl.BlockSpec((1, CHUNK, D_INNER), lambda bi, c: (bi, cidx(c), 0)),
                  pl.BlockSpec((1, CHUNK, gn), lambda bi, c: (bi, cidx(c), bcol)),
                  pl.BlockSpec((1, CHUNK, gn), lambda bi, c: (bi, cidx(c), bcol + 1)),
                  pl.BlockSpec((1, CHUNK, LANES), lambda bi, c: (bi, cidx(c), 0)),
                  const(dt_bias), const(a_log), const(expand)],
        out_specs=pl.BlockSpec((1, CHUNK, D_INNER), lambda bi, c: (bi, cidx(c), 0)),
        out_shape=jax.ShapeDtypeStruct((b, s, D_INNER), BF16),
        scratch_shapes=[pltpu.VMEM((D_STATE, D_INNER), F32)],
        compiler_params=_cparams(("parallel", "arbitrary")),
        name="ssd_bwd" if reverse else "ssd_fwd",
    )(conv_out, conv_out, conv_out, dt, dt_bias, a_log, expand)


def _mix_body(x_ref, o_ref, yf_ref, yb_ref, xs_ref, z_ref, g_ref,
              woa_ref, wos_ref, wout_ref, dsk_ref, nw_ref, lg_ref, lb_ref, x1_ref, x1p_ref):
    attn = _dot(o_ref[...], woa_ref[...])
    y = yf_ref[...].astype(F32) + yb_ref[...].astype(F32) + dsk_ref[...] * xs_ref[...].astype(F32)
    y = y * z_ref[...].astype(F32)
    yn = y * lax.rsqrt(jnp.mean(y * y, axis=-1, keepdims=True) + RMS_EPS) * nw_ref[...]
    ssd = _dot(yn.astype(BF16), wos_ref[...])
    g = g_ref[...].astype(F32)
    mixed_in = g[:, :D_MODEL] * attn + g[:, D_MODEL:] * ssd
    mixed = _dot(mixed_in.astype(BF16), wout_ref[...])
    x1 = _layer_norm_rows(DN_ALPHA * x_ref[...] + mixed, lg_ref[...], lb_ref[...])
    x1_ref[...] = x1
    x1p_ref[...] = _pack_rows(x1)


def _mix(x2d, o, yf, yb, conv_out, z, g, woa, wos, wout, dsk, nw, lg, lb, tm=256):
    t = x2d.shape[0]
    const = lambda w: pl.BlockSpec(w.shape, lambda i: (0, 0))
    row = lambda n: pl.BlockSpec((tm, n), lambda i: (i, 0))
    return pl.pallas_call(
        _mix_body,
        grid=(t // tm,),
        in_specs=[row(D_MODEL), row(D_MODEL), row(D_INNER), row(D_INNER), row(D_INNER), row(D_INNER),
                  row(2 * D_MODEL), const(woa), const(wos), const(wout), const(dsk), const(nw),
                  const(lg), const(lb)],
        out_specs=[row(D_MODEL), row(PACKED_W)],
        out_shape=[jax.ShapeDtypeStruct((t, D_MODEL), F32), jax.ShapeDtypeStruct((t, PACKED_W), jnp.uint32)],
        compiler_params=_cparams(("parallel",)),
        name="mix_ln1",
    )(x2d, o, yf, yb, conv_out, z, g, woa, wos, wout, dsk, nw, lg, lb)


def _router_body(x_ref, wh_ref, wl_ref, bias_ref, su_ref, idx_ref, wts_ref, pos_ref, cnt_ref, carry_ref):
    @pl.when(pl.program_id(0) == 0)
    def _():
        carry_ref[...] = jnp.zeros_like(carry_ref)

    tm = x_ref.shape[0]
    x = x_ref[...]
    xh = x.astype(BF16)
    xl = (x - xh.astype(F32)).astype(BF16)
    wh = wh_ref[...]
    logits = _dot_nt(wh, xh) + _dot_nt(wh, xl) + _dot_nt(wl_ref[...], xh)
    scores = jax.nn.sigmoid(logits)
    choice = scores + bias_ref[...]
    gsz = N_EXPERTS // N_EXPERT_GROUPS
    shp = (N_EXPERT_GROUPS, gsz, tm)
    ch = choice.reshape(shp)
    sc = scores.reshape(shp)
    jio = lax.broadcasted_iota(jnp.int32, shp, 1)
    gio = lax.broadcasted_iota(jnp.int32, shp, 0)
    eio = gio * gsz + jio
    neg = -jnp.inf

    m1 = jnp.max(ch, axis=1, keepdims=True)
    i1 = jnp.min(jnp.where(ch == m1, jio, gsz), axis=1, keepdims=True)
    m2 = jnp.max(jnp.where(jio == i1, neg, ch), axis=1, keepdims=True)
    gs = m1 + m2
    gio1 = lax.broadcasted_iota(jnp.int32, (N_EXPERT_GROUPS, 1, tm), 0)
    sel = jnp.zeros(gs.shape, jnp.bool_)
    cur = gs
    for _ in range(TOPK_GROUPS):
        m = jnp.max(cur, axis=0, keepdims=True)
        gi = jnp.min(jnp.where(cur == m, gio1, N_EXPERT_GROUPS), axis=0, keepdims=True)
        hit = gio1 == gi
        sel = jnp.logical_or(sel, hit)
        cur = jnp.where(hit, neg, cur)
    masked = jnp.where(sel, ch, neg)

    def red(op, v):
        return op(op(v, axis=1, keepdims=True), axis=0, keepdims=True)

    hits, idxs, ws = [], [], []
    for _ in range(TOP_K):
        m = red(jnp.max, masked)
        ei = red(jnp.min, jnp.where(masked == m, eio, N_EXPERTS))
        hit = eio == ei
        hits.append(hit)
        idxs.append(ei)
        ws.append(red(jnp.sum, jnp.where(hit, sc, 0.0)))
        masked = jnp.where(hit, neg, masked)
    wsum = ws[0]
    for k in range(1, TOP_K):
        wsum = wsum + ws[k]

    onehot = hits[0]
    for k in range(1, TOP_K):
        onehot = jnp.logical_or(onehot, hits[k])
    oh = onehot.astype(F32).reshape(N_EXPERTS, tm)
    before = _dot(oh.astype(BF16), su_ref[...]) + carry_ref[...]
    before3 = before.reshape(shp)
    for k in range(TOP_K):
        idx_ref[k:k + 1, :] = idxs[k].reshape(1, tm)
        wts_ref[k:k + 1, :] = (ws[k] / wsum * ROUTED_SCALE).reshape(1, tm)
        pos_ref[k:k + 1, :] = red(jnp.sum, jnp.where(hits[k], before3, 0.0)).reshape(1, tm).astype(jnp.int32)
    carry_ref[...] = carry_ref[...] + jnp.sum(oh, axis=1, keepdims=True)
    cnt_ref[...] = jnp.broadcast_to(carry_ref[...], cnt_ref.shape)


def _router(x1, wr_hi, wr_lo, bias, su, tm=512):
    t = x1.shape[0]
    const = lambda w: pl.BlockSpec(w.shape, lambda i: (0, 0))
    out_tok = pl.BlockSpec((TOP_K, tm), lambda i: (0, i))
    return pl.pallas_call(
        _router_body,
        grid=(t // tm,),
        in_specs=[pl.BlockSpec((tm, D_MODEL), lambda i: (i, 0)), const(wr_hi), const(wr_lo), const(bias),
                  const(su)],
        out_specs=[out_tok, out_tok, out_tok, pl.BlockSpec((N_EXPERTS, LANES), lambda i: (0, 0))],
        out_shape=[jax.ShapeDtypeStruct((TOP_K, t), jnp.int32),
                   jax.ShapeDtypeStruct((TOP_K, t), F32),
                   jax.ShapeDtypeStruct((TOP_K, t), jnp.int32),
                   jax.ShapeDtypeStruct((N_EXPERTS, LANES), F32)],
        scratch_shapes=[pltpu.VMEM((N_EXPERTS, 1), F32)],
        compiler_params=_cparams(("arbitrary",)),
        name="router",
    )(x1, wr_hi, wr_lo, bias, su)


def _expert_body(be_ref, x_ref, wg_ref, wu_ref, wd_ref, y_ref):
    hi, lo = _unpack_rows(x_ref[...])
    xb = jnp.concatenate([hi.astype(BF16), lo.astype(BF16)], axis=1)
    h = _silu(_dot(xb, wg_ref[0])) * _dot(xb, wu_ref[0])
    y_ref[...] = _pack_rows(_dot(h.astype(BF16), wd_ref[0]))


def _experts(block_e, xs, wg, wu, wd):
    rows = xs.shape[0]
    nb = rows // EXPERT_BLOCK
    grid_spec = pltpu.PrefetchScalarGridSpec(
        num_scalar_prefetch=1,
        grid=(nb,),
        in_specs=[pl.BlockSpec((EXPERT_BLOCK, PACKED_W), lambda b, be: (b, 0)),
                  pl.BlockSpec((1, D_MODEL, D_EXPERT), lambda b, be: (be[b], 0, 0)),
                  pl.BlockSpec((1, D_MODEL, D_EXPERT), lambda b, be: (be[b], 0, 0)),
                  pl.BlockSpec((1, D_EXPERT, D_MODEL), lambda b, be: (be[b], 0, 0))],
        out_specs=pl.BlockSpec((EXPERT_BLOCK, PACKED_W), lambda b, be: (b, 0)),
    )
    return pl.pallas_call(
        _expert_body,
        grid_spec=grid_spec,
        out_shape=jax.ShapeDtypeStruct((rows, PACKED_W), jnp.uint32),
        compiler_params=_cparams(("parallel",)),
        name="moe_experts",
    )(block_e, xs, wg, wu, wd)


def _final_body(x1_ref, wt_ref, g_ref, wsg_ref, wsu_ref, wsd_ref, lg_ref, lb_ref, out_ref):
    x1 = x1_ref[...]
    xb = x1.astype(BF16)
    h = _silu(_dot(xb, wsg_ref[...])) * _dot(xb, wsu_ref[...])
    acc = DN_ALPHA * x1 + _dot(h.astype(BF16), wsd_ref[...])
    wt = wt_ref[...]
    acc_hi, acc_lo = acc[:, :PACKED_W], acc[:, PACKED_W:]
    for k in range(TOP_K):
        hi, lo = _unpack_rows(g_ref[k])
        acc_hi = acc_hi + wt[:, k:k + 1] * hi
        acc_lo = acc_lo + wt[:, k:k + 1] * lo
    acc = jnp.concatenate([acc_hi, acc_lo], axis=1)
    out_ref[...] = _layer_norm_rows(acc, lg_ref[...], lb_ref[...])


def _final(x1, wt_rows, g, wsg, wsu, wsd, lg, lb, tm=256):
    t = x1.shape[0]
    const = lambda w: pl.BlockSpec(w.shape, lambda i: (0, 0))
    return pl.pallas_call(
        _final_body,
        grid=(t // tm,),
        in_specs=[pl.BlockSpec((tm, D_MODEL), lambda i: (i, 0)),
                  pl.BlockSpec((tm, TOP_K), lambda i: (i, 0)),
                  pl.BlockSpec((TOP_K, tm, PACKED_W), lambda i: (0, i, 0)),
                  const(wsg), const(wsu), const(wsd), const(lg), const(lb)],
        out_specs=pl.BlockSpec((tm, D_MODEL), lambda i: (i, 0)),
        out_shape=jax.ShapeDtypeStruct((t, D_MODEL), F32),
        compiler_params=_cparams(("parallel",)),
        name="moe_combine_ln2",
    )(x1, wt_rows, g, wsg, wsu, wsd, lg, lb)


SC_CORES = 2
SC_SUBCORES = 16
SC_CHUNK = 128


def _sc_scatter_rows(src, dest, pad_rows, rows):
    t, w = src.shape
    workers = SC_CORES * SC_SUBCORES
    per_w = t // workers
    steps = per_w // SC_CHUNK
    pad_steps = pad_rows.shape[0] // (workers * SC_CHUNK)
    assert steps * SC_CHUNK * workers == t and pad_steps * SC_CHUNK * workers == pad_rows.shape[0]
    mesh = plsc.VectorSubcoreMesh(core_axis_name="c", subcore_axis_name="s")
    idx_tok = dest.reshape(TOP_K, workers, steps, SC_CHUNK).transpose(1, 2, 0, 3)
    idx_pad = pad_rows.reshape(workers, pad_steps, SC_CHUNK)
    zeros = jnp.zeros((SC_CHUNK, w), src.dtype)

    def body(src_hbm, tok_hbm, pad_hbm, zero_hbm, out_hbm, tok_v, pad_v, rows_v, sem):
        wid = lax.axis_index("s") * SC_CORES + lax.axis_index("c")
        pltpu.sync_copy(tok_hbm.at[wid], tok_v)
        pltpu.sync_copy(pad_hbm.at[wid], pad_v)

        @pl.loop(0, steps)
        def _(i):
            off = pl.multiple_of(wid * per_w + i * SC_CHUNK, SC_CHUNK)
            pltpu.sync_copy(src_hbm.at[pl.ds(off, SC_CHUNK)], rows_v)
            copies = [pltpu.async_copy(rows_v, out_hbm.at[tok_v.at[i, k]], sem) for k in range(TOP_K)]
            for c in copies:
                c.wait()

        pltpu.sync_copy(zero_hbm, rows_v)

        @pl.loop(0, pad_steps)
        def _(i):
            pltpu.async_copy(rows_v, out_hbm.at[pad_v.at[i]], sem).wait()

    return pl.kernel(
        body, mesh=mesh,
        out_type=jax.ShapeDtypeStruct((rows, w), src.dtype),
        scratch_types=[pltpu.VMEM((steps, TOP_K, SC_CHUNK), jnp.int32), pltpu.VMEM((pad_steps, SC_CHUNK), jnp.int32),
                       pltpu.VMEM((SC_CHUNK, w), src.dtype), pltpu.SemaphoreType.DMA],
    )(src, idx_tok, idx_pad, zeros)


def _sc_gather_rows(table, idx):
    n = idx.shape[0]
    w = table.shape[1]
    workers = SC_CORES * SC_SUBCORES
    per_w = n // workers
    steps = per_w // SC_CHUNK
    assert steps * SC_CHUNK * workers == n
    mesh = plsc.VectorSubcoreMesh(core_axis_name="c", subcore_axis_name="s")

    def body(table_hbm, idx_hbm, out_hbm, idx_v, rows_v, sem):
        wid = lax.axis_index("s") * SC_CORES + lax.axis_index("c")
        pltpu.sync_copy(idx_hbm.at[wid], idx_v)

        @pl.loop(0, steps)
        def _(i):
            off = pl.multiple_of(wid * per_w + i * SC_CHUNK, SC_CHUNK)
            pltpu.async_copy(table_hbm.at[idx_v.at[i]], rows_v, sem).wait()
            pltpu.sync_copy(rows_v, out_hbm.at[pl.ds(off, SC_CHUNK)])

    return pl.kernel(
        body, mesh=mesh,
        out_type=jax.ShapeDtypeStruct((n, w), table.dtype),
        scratch_types=[pltpu.VMEM((steps, SC_CHUNK), jnp.int32), pltpu.VMEM((SC_CHUNK, w), table.dtype),
                       pltpu.SemaphoreType.DMA],
    )(table, idx.reshape(workers, steps, SC_CHUNK))


def _dest_body(ps_ref, idx_ref, pos_ref, out_ref):
    idx = idx_ref[...]
    acc = pos_ref[...]
    for e in range(N_EXPERTS):
        acc = acc + jnp.where(idx == e, ps_ref[e], 0)
    out_ref[...] = acc


def _dest_rows(pad_start, idx, pos, tn=2048):
    t = idx.shape[1]
    tn = min(tn, t)
    tok = pl.BlockSpec((TOP_K, tn), lambda i, ps: (0, i))
    grid_spec = pltpu.PrefetchScalarGridSpec(
        num_scalar_prefetch=1, grid=(t // tn,), in_specs=[tok, tok], out_specs=tok)
    return pl.pallas_call(
        _dest_body,
        grid_spec=grid_spec,
        out_shape=jax.ShapeDtypeStruct((TOP_K, t), jnp.int32),
        compiler_params=_cparams(("parallel",)),
        name="moe_dest",
    )(pad_start, idx, pos)


def _rot_cols(w):
    half = w.shape[-1] // 2
    return jnp.concatenate([-w[..., half:], w[..., :half]], axis=-1)


def _prepare(p, seq_lens):
    f = {}
    w_in = p["w_in"]
    o = np.cumsum((0, Q_LORA, KV_LORA, QK_ROPE, D_INNER, CONV_DIM, SSD_HEADS, SSD_HEADS, D_MODEL, D_MODEL))
    seg = lambda i: w_in[:, o[i]:o[i + 1]]
    zeros = lambda n: jnp.zeros((D_MODEL, n), F32)
    f["wq"] = jnp.concatenate([seg(0), seg(1), seg(2), _rot_cols(seg(2)), zeros(QKV_W - KROPE_OFF - 2 * QK_ROPE)],
                              axis=1).astype(BF16)
    f["wz"] = seg(3).astype(BF16)
    f["wx"] = seg(4).astype(BF16)
    f["wdt"] = jnp.concatenate([seg(5), seg(6), zeros(LANES - 2 * SSD_HEADS)], axis=1).astype(BF16)
    f["wg"] = jnp.concatenate([seg(7), seg(8)], axis=1).astype(BF16)

    f["qn"] = p["q_norm_w"].reshape(1, Q_LORA)
    f["kvn"] = p["kv_norm_w"].reshape(1, KV_LORA)
    w_uq = p["w_uq"]
    nope, rope = w_uq[..., :QK_NOPE], w_uq[..., QK_NOPE:]
    padq = HEAD_PAD - QK_NOPE - QK_ROPE
    zq = lambda n: jnp.zeros((Q_LORA, N_HEADS, n), F32)
    wq_a = jnp.concatenate([nope, rope, zq(padq)], axis=-1).reshape(Q_LORA, N_HEADS * HEAD_PAD)
    wq_b = jnp.concatenate([zq(QK_NOPE), _rot_cols(rope), zq(padq)], axis=-1).reshape(Q_LORA, N_HEADS * HEAD_PAD)
    f["w_q"] = jnp.concatenate([wq_a, wq_b], axis=1).astype(BF16)
    w_ukv = p["w_ukv"]
    wk_t = jnp.transpose(w_ukv[..., :QK_NOPE], (1, 2, 0))
    eye = jnp.eye(QK_ROPE, dtype=F32)
    rope_rows = jnp.concatenate([jnp.zeros((QK_ROPE, KV_LORA), F32), eye, eye,
                                 jnp.zeros((QK_ROPE, LANES - 2 * QK_ROPE), F32)], axis=1)
    wk = jnp.concatenate([
        jnp.concatenate([wk_t, jnp.zeros((N_HEADS, QK_NOPE, LANES), F32)], axis=2),
        jnp.broadcast_to(rope_rows, (N_HEADS, QK_ROPE, KV_LORA + LANES)),
        jnp.zeros((N_HEADS, padq, KV_LORA + LANES), F32)], axis=1)
    f["w_k"] = wk.reshape(N_HEADS * HEAD_PAD, KV_LORA + LANES).astype(BF16)
    w_v = jnp.concatenate([w_ukv[..., QK_NOPE:], jnp.zeros((KV_LORA, N_HEADS, HEAD_PAD - V_DIM), F32)], axis=-1)
    f["w_v"] = w_v.reshape(KV_LORA, N_HEADS * HEAD_PAD).astype(BF16)
    f["v_one"] = (jnp.arange(N_HEADS * HEAD_PAD) % HEAD_PAD == V_DIM).astype(F32).reshape(1, -1)

    scale = (QK_NOPE + QK_ROPE) ** -0.5 * math.log2(math.e)
    f["rope"] = {}
    for s in sorted(set(seq_lens)):
        inv_freq = ROPE_THETA ** (-jnp.arange(0, QK_ROPE, 2, dtype=F32) / QK_ROPE)
        ang = jnp.arange(s, dtype=F32)[:, None] * inv_freq[None, :]
        cos, sin = jnp.cos(ang), jnp.sin(ang)
        one, zero = jnp.ones((s, QK_NOPE), F32), jnp.zeros((s, QK_NOPE), F32)
        zp = jnp.zeros((s, padq), F32)
        cq_t = scale * jnp.concatenate([one, cos, cos, zp], axis=1)
        sq_t = scale * jnp.concatenate([zero, sin, sin, zp], axis=1)
        tk_t = jnp.concatenate([cos, cos, sin, sin, jnp.zeros((s, LANES - 2 * QK_ROPE), F32)], axis=1)
        f["rope"][s] = (cq_t, sq_t, tk_t)

    f["conv_w"] = jnp.concatenate([p["conv_w"], jnp.zeros((8 - CONV_WIDTH, CONV_DIM), F32)], axis=0)
    f["conv_b"] = p["conv_b"].reshape(1, CONV_DIM)
    pad_h = jnp.zeros((LANES - 2 * SSD_HEADS,), F32)
    f["dt_bias"] = jnp.concatenate([p["dt_bias_f"], p["dt_bias_b"], pad_h]).reshape(1, LANES)
    f["a_log"] = jnp.concatenate([p["a_log_f"], p["a_log_b"], pad_h]).reshape(1, LANES)
    head_of_lane = jnp.arange(D_INNER, dtype=jnp.int32) // SSD_HEADDIM
    lane_id = jnp.arange(LANES, dtype=jnp.int32)[:, None]
    f["expand_f"] = (lane_id == head_of_lane[None, :]).astype(BF16)
    f["expand_b"] = (lane_id == head_of_lane[None, :] + SSD_HEADS).astype(BF16)
    f["d_skip"] = jnp.repeat(p["d_skip"], SSD_HEADDIM).reshape(1, D_INNER)
    f["ssd_nw"] = p["ssd_norm_w"].reshape(1, D_INNER)
    f["w_oa"] = p["w_o_attn"].astype(BF16)
    f["w_os"] = p["w_o_ssd"].astype(BF16)
    f["w_out"] = p["w_out"].astype(BF16)
    f["ln1_g"] = p["ln1_g"].reshape(1, D_MODEL)
    f["ln1_b"] = p["ln1_b"].reshape(1, D_MODEL)

    wr_t = p["w_router"].T
    wr_hi = wr_t.astype(BF16)
    f["wr_hi"] = wr_hi
    f["wr_lo"] = (wr_t - wr_hi.astype(F32)).astype(BF16)
    f["r_bias"] = p["router_bias"].reshape(N_EXPERTS, 1)
    f["w_gate"] = p["w_gate"].astype(BF16)
    f["w_up"] = p["w_up"].astype(BF16)
    f["w_down"] = p["w_down"].astype(BF16)
    f["ws_gate"] = p["ws_gate"].astype(BF16)
    f["ws_up"] = p["ws_up"].astype(BF16)
    f["ws_down"] = p["ws_down"].astype(BF16)
    f["ln2_g"] = p["ln2_g"].reshape(1, D_MODEL)
    f["ln2_b"] = p["ln2_b"].reshape(1, D_MODEL)
    return f


ROUTER_TILE = 512


def _layer(x, f):
    b, s, d = x.shape
    t = b * s
    x2d = x.reshape(t, d)
    qkv, z, xbc, dt, g = _in_proj(x2d, f["wq"], f["wz"], f["wx"], f["wdt"], f["wg"])

    cq_t, sq_t, tk_t = f["rope"][s]
    q, kt, v = _mla_prep(qkv.reshape(b, s, QKV_W), f["qn"], f["kvn"], f["w_q"], f["w_k"], f["w_v"], f["v_one"],
                         cq_t, sq_t, tk_t)
    o = _attention(q, kt, v)

    conv_out = _conv(xbc.reshape(b, s, CONV_DIM), f["conv_w"], f["conv_b"])
    dt3 = dt.reshape(b, s, LANES)
    y_f = _ssd(conv_out, dt3, f["dt_bias"], f["a_log"], f["expand_f"], reverse=False)
    y_b = _ssd(conv_out, dt3, f["dt_bias"], f["a_log"], f["expand_b"], reverse=True)

    x1, x1p = _mix(x2d, o.reshape(t, -1), y_f.reshape(t, -1), y_b.reshape(t, -1), conv_out.reshape(t, -1), z, g,
              f["w_oa"], f["w_os"], f["w_out"], f["d_skip"], f["ssd_nw"], f["ln1_g"], f["ln1_b"])

    su = (jnp.arange(ROUTER_TILE)[:, None] < jnp.arange(ROUTER_TILE)[None, :]).astype(BF16)
    idx, wts, pos, cnt = _router(x1, f["wr_hi"], f["wr_lo"], f["r_bias"], su, tm=ROUTER_TILE)

    counts = cnt[:, 0].astype(jnp.int32)
    padded = (counts + EXPERT_BLOCK - 1) // EXPERT_BLOCK * EXPERT_BLOCK
    pad_end = jnp.cumsum(padded)
    pad_start = pad_end - padded
    nb = t * TOP_K // EXPERT_BLOCK + N_EXPERTS
    n_used = pad_end[-1] // EXPERT_BLOCK
    blk_row = jnp.minimum(jnp.arange(nb, dtype=jnp.int32), n_used - 1) * EXPERT_BLOCK
    block_e = jnp.minimum(jnp.sum((blk_row[:, None] >= pad_end[None, :]).astype(jnp.int32), axis=1), N_EXPERTS - 1)

    seg_len = jnp.concatenate([padded - counts, (nb * EXPERT_BLOCK - pad_end[-1]).reshape(1)])
    seg_first = jnp.concatenate([pad_start + counts, pad_end[-1:]])
    seg_end = jnp.cumsum(seg_len)
    j = jnp.arange(N_EXPERTS * EXPERT_BLOCK, dtype=jnp.int32)
    onehot = (jnp.sum((j[:, None] >= seg_end[None, :]).astype(jnp.int32), axis=1)[:, None]
              == jnp.arange(N_EXPERTS + 1, dtype=jnp.int32)[None, :])
    pad_rows = j + jnp.sum(jnp.where(onehot, (seg_first - (seg_end - seg_len))[None, :], 0), axis=1)

    dest = _dest_rows(pad_start, idx, pos)
    xs = _sc_scatter_rows(x1p, dest, pad_rows.astype(jnp.int32), nb * EXPERT_BLOCK)
    ys = _experts(block_e, xs, f["w_gate"], f["w_up"], f["w_down"])
    g = _sc_gather_rows(ys, dest.reshape(TOP_K * t)).reshape(TOP_K, t, PACKED_W)
    y = _final(x1, wts.T, g, f["ws_gate"], f["ws_up"], f["ws_down"], f["ln2_g"], f["ln2_b"])
    return y.reshape(b, s, d)


def kernel(x_prompt, x_sample, w_in, q_norm_w, kv_norm_w, w_uq, w_ukv, w_o_attn, conv_w, conv_b, dt_bias_f,
           dt_bias_b, a_log_f, a_log_b, d_skip, ssd_norm_w, w_o_ssd, w_out, ln1_g, ln1_b, w_router, router_bias,
           w_gate, w_up, w_down, ws_gate, ws_up, ws_down, ln2_g, ln2_b):
    params = dict(w_in=w_in, q_norm_w=q_norm_w, kv_norm_w=kv_norm_w, w_uq=w_uq, w_ukv=w_ukv, w_o_attn=w_o_attn,
                  conv_w=conv_w, conv_b=conv_b, dt_bias_f=dt_bias_f, dt_bias_b=dt_bias_b, a_log_f=a_log_f,
                  a_log_b=a_log_b, d_skip=d_skip, ssd_norm_w=ssd_norm_w, w_o_ssd=w_o_ssd, w_out=w_out,
                  ln1_g=ln1_g, ln1_b=ln1_b, w_router=w_router, router_bias=router_bias, w_gate=w_gate,
                  w_up=w_up, w_down=w_down, ws_gate=ws_gate, ws_up=ws_up, ws_down=ws_down, ln2_g=ln2_g,
                  ln2_b=ln2_b)
    assert w_in.shape[0] == DEPTH == 1
    params = {k: v[0] for k, v in params.items()}
    f = _prepare(params, (x_prompt.shape[1], x_sample.shape[1]))
    return (_layer(x_prompt, f), _layer(x_sample, f))
```

```python
import functools
import math

import jax
import jax.numpy as jnp
import numpy as np
from jax import lax
from jax.experimental import pallas as pl
from jax.experimental.pallas import tpu as pltpu
from jax.experimental.pallas import tpu_sc as plsc

F32 = jnp.float32
BF16 = jnp.bfloat16

D_MODEL = 1024
DEPTH = 1
N_HEADS = 16
QK_NOPE = 64
QK_ROPE = 32
V_DIM = 64
Q_LORA = 384
KV_LORA = 256
ROPE_THETA = 10000.0
D_INNER = 2 * D_MODEL
SSD_HEADDIM = 64
SSD_HEADS = D_INNER // SSD_HEADDIM
SSD_GROUPS = 4
D_STATE = 128
CONV_WIDTH = 5
CONV_DIM = D_INNER + 2 * SSD_GROUPS * D_STATE
CHUNK = 128
N_EXPERTS = 64
TOP_K = 8
N_EXPERT_GROUPS = 8
TOPK_GROUPS = 4
D_EXPERT = D_MODEL // 4
D_SHARED = D_MODEL // 4
ROUTED_SCALE = 2.5
LN_EPS = 1e-5
RMS_EPS = 1e-6
DN_ALPHA = (2 * DEPTH) ** 0.25

LANES = 128
HEAD_PAD = 128
QKV_W = 768
KROPE_OFF = Q_LORA + KV_LORA
EXPERT_BLOCK = 512
PACKED_W = D_MODEL // 2
VMEM_LIMIT = 56 * 1024 * 1024


def _cparams(sem, vmem=VMEM_LIMIT):
    return pltpu.CompilerParams(dimension_semantics=sem, vmem_limit_bytes=vmem)


def _dot(a, b):
    return jnp.dot(a, b, preferred_element_type=F32)


def _dot_nt(a, b):
    return lax.dot_general(a, b, (((1,), (1,)), ((), ())), preferred_element_type=F32)


def _split3(a):
    h = a.astype(BF16)
    r = a - h.astype(F32)
    m = r.astype(BF16)
    l = (r - m.astype(F32)).astype(BF16)
    return h, m, l


def _dot_split_lhs(a_f32, b_bf16):
    h, m, l = _split3(a_f32)
    return _dot(h, b_bf16) + _dot(m, b_bf16) + _dot(l, b_bf16)


def _silu(x):
    return x * jax.nn.sigmoid(x)


def _pack_rows(x):
    n = x.shape[1] // 2
    hi = lax.bitcast_convert_type(x[:, :n].astype(BF16).astype(F32), jnp.uint32)
    lo = lax.bitcast_convert_type(x[:, n:].astype(BF16).astype(F32), jnp.uint32)
    return hi | (lo >> 16)


def _unpack_rows(w):
    hi = lax.bitcast_convert_type(w & jnp.uint32(0xFFFF0000), F32)
    lo = lax.bitcast_convert_type(w << 16, F32)
    return hi, lo


def _layer_norm_rows(r, g, b):
    mu = jnp.mean(r, axis=-1, keepdims=True)
    d = r - mu
    var = jnp.mean(d * d, axis=-1, keepdims=True)
    return d * lax.rsqrt(var + LN_EPS) * g + b


def _inproj_body(x_ref, wq_ref, wz_ref, wx_ref, wdt_ref, wg_ref,
                 qkv_ref, z_ref, xbc_ref, dt_ref, g_ref):
    xb = x_ref[...].astype(BF16)
    qkv_ref[...] = _dot(xb, wq_ref[...]).astype(BF16)
    z_ref[...] = _silu(_dot(xb, wz_ref[...])).astype(BF16)
    xbc_ref[...] = _dot(xb, wx_ref[...]).astype(BF16)
    dt_ref[...] = _dot(xb, wdt_ref[...])
    g_ref[...] = jax.nn.sigmoid(_dot(xb, wg_ref[...])).astype(BF16)


def _in_proj(x2d, wq, wz, wx, wdt, wg, tm=512):
    t = x2d.shape[0]
    const = lambda w: pl.BlockSpec(w.shape, lambda i: (0, 0), pipeline_mode=pl.Buffered(1))
    row = lambda n: pl.BlockSpec((tm, n), lambda i: (i, 0))
    return pl.pallas_call(
        _inproj_body,
        grid=(t // tm,),
        in_specs=[row(D_MODEL), const(wq), const(wz), const(wx), const(wdt), const(wg)],
        out_specs=[row(QKV_W), row(D_INNER), row(CONV_DIM), row(LANES), row(2 * D_MODEL)],
        out_shape=[jax.ShapeDtypeStruct((t, QKV_W), BF16),
                   jax.ShapeDtypeStruct((t, D_INNER), BF16),
                   jax.ShapeDtypeStruct((t, CONV_DIM), BF16),
                   jax.ShapeDtypeStruct((t, LANES), F32),
                   jax.ShapeDtypeStruct((t, 2 * D_MODEL), BF16)],
        compiler_params=_cparams(("parallel",)),
        name="in_proj",
    )(x2d, wq, wz, wx, wdt, wg)


def _mla_prep_body(qkv_ref, qn_ref, kvn_ref, wq_ref, wk_ref, wv_ref, vone_ref, cq_ref, sq_ref, tk_ref,
                   q_ref, kt_ref, v_ref):
    qkv = qkv_ref[0].astype(F32)
    cq = qkv[:, :Q_LORA]
    cq = cq * lax.rsqrt(jnp.mean(cq * cq, axis=-1, keepdims=True) + RMS_EPS) * qn_ref[...]
    ckv = qkv[:, Q_LORA:KROPE_OFF]
    ckv = ckv * lax.rsqrt(jnp.mean(ckv * ckv, axis=-1, keepdims=True) + RMS_EPS) * kvn_ref[...]
    ckv_b = ckv.astype(BF16)
    qq = _dot(cq.astype(BF16), wq_ref[...])
    half = N_HEADS * HEAD_PAD
    cq_t = cq_ref[...]
    sq_t = sq_ref[...]
    for h in range(N_HEADS):
        lo = h * HEAD_PAD
        q_ref[0, :, lo:lo + HEAD_PAD] = (qq[:, lo:lo + HEAD_PAD] * cq_t
                                          + qq[:, half + lo:half + lo + HEAD_PAD] * sq_t).astype(BF16)
    kr = (qkv[:, KROPE_OFF:] * tk_ref[...]).astype(BF16)
    e = jnp.concatenate([ckv_b, kr], axis=1)
    kt_ref[0] = _dot_nt(wk_ref[...], e).astype(BF16)
    v_ref[0] = (_dot(ckv_b, wv_ref[...]) + vone_ref[...]).astype(BF16)


def _mla_prep(qkv, qn, kvn, wq, wk, wv, vone, cq_t, sq_t, tk_t, ts=512):
    b, s, _ = qkv.shape
    const = lambda w: pl.BlockSpec(w.shape, lambda bi, i: (0, 0))
    tab = pl.BlockSpec((ts, LANES), lambda bi, i: (i, 0))
    hw = N_HEADS * HEAD_PAD
    return pl.pallas_call(
        _mla_prep_body,
        grid=(b, s // ts),
        in_specs=[pl.BlockSpec((1, ts, QKV_W), lambda bi, i: (bi, i, 0)),
                  const(qn), const(kvn), const(wq), const(wk), const(wv), const(vone), tab, tab, tab],
        out_specs=[pl.BlockSpec((1, ts, hw), lambda bi, i: (bi, i, 0)),
                   pl.BlockSpec((1, hw, ts), lambda bi, i: (bi, 0, i)),
                   pl.BlockSpec((1, ts, hw), lambda bi, i: (bi, i, 0))],
        out_shape=[jax.ShapeDtypeStruct((b, s, hw), BF16),
                   jax.ShapeDtypeStruct((b, hw, s), BF16),
                   jax.ShapeDtypeStruct((b, s, hw), BF16)],
        compiler_params=_cparams(("parallel", "parallel")),
        name="mla_prep",
    )(qkv, qn, kvn, wq, wk, wv, vone, cq_t, sq_t, tk_t)


def _attn_body(q_ref, kt_ref, v_ref, o_ref):
    tq = q_ref.shape[1]
    for r in range(tq // ATTN_ROWS):
        rows = slice(r * ATTN_ROWS, (r + 1) * ATTN_ROWS)
        outs = []
        for hh in range(2):
            sl = slice(hh * HEAD_PAD, (hh + 1) * HEAD_PAD)
            s = _dot(q_ref[0, rows, sl], kt_ref[0, sl, :])
            m = jnp.max(s, axis=-1, keepdims=True)
            p = jnp.exp2(s - m).astype(BF16)
            o = _dot(p, v_ref[0, :, sl])
            outs.append(o / o[:, V_DIM:V_DIM + 1])
        lane = lax.broadcasted_iota(jnp.int32, outs[0].shape, 1)
        o_ref[0, rows, :] = jnp.where(lane < V_DIM, outs[0], pltpu.roll(outs[1], V_DIM, axis=1)).astype(BF16)


ATTN_ROWS = 256


def _attention(q, kt, v, tq=1024):
    b, s, _ = q.shape
    pairs = N_HEADS // 2
    return pl.pallas_call(
        _attn_body,
        grid=(b, pairs, s // tq),
        in_specs=[pl.BlockSpec((1, tq, 2 * HEAD_PAD), lambda bi, j, i: (bi, i, j)),
                  pl.BlockSpec((1, 2 * HEAD_PAD, s), lambda bi, j, i: (bi, j, 0)),
                  pl.BlockSpec((1, s, 2 * HEAD_PAD), lambda bi, j, i: (bi, 0, j))],
        out_specs=pl.BlockSpec((1, tq, 2 * V_DIM), lambda bi, j, i: (bi, i, j)),
        out_shape=jax.ShapeDtypeStruct((b, s, N_HEADS * V_DIM), BF16),
        compiler_params=_cparams(("parallel", "parallel", "arbitrary")),
        name="attention",
    )(q, kt, v)


CONV_HALO = 16


def _conv_body(prev_ref, main_ref, next_ref, w_ref, b_ref, o_ref, ext_ref):
    i = pl.program_id(1)
    n = pl.num_programs(1)
    ts = main_ref.shape[1]
    prev = prev_ref[0].astype(F32)
    nxt = next_ref[0].astype(F32)
    ext_ref[0:CONV_HALO, :] = jnp.where(i > 0, prev, 0.0)
    ext_ref[CONV_HALO:CONV_HALO + ts, :] = main_ref[0].astype(F32)
    ext_ref[CONV_HALO + ts:, :] = jnp.where(i < n - 1, nxt, 0.0)
    acc = b_ref[...]
    for k in range(CONV_WIDTH):
        off = CONV_HALO - CONV_WIDTH // 2 + k
        acc = acc + ext_ref[off:off + ts, :] * w_ref[k:k + 1, :]
    o_ref[0] = _silu(acc).astype(BF16)


def _conv(xbc, w8, bias, ts=1024, tc=1024):
    b, s, c = xbc.shape
    r = ts // CONV_HALO
    last = s // CONV_HALO - 1
    return pl.pallas_call(
        _conv_body,
        grid=(b, s // ts, c // tc),
        in_specs=[pl.BlockSpec((1, CONV_HALO, tc), lambda bi, i, j: (bi, jnp.maximum(i * r - 1, 0), j)),
                  pl.BlockSpec((1, ts, tc), lambda bi, i, j: (bi, i, j)),
                  pl.BlockSpec((1, CONV_HALO, tc), lambda bi, i, j: (bi, jnp.minimum((i + 1) * r, last), j)),
                  pl.BlockSpec((8, tc), lambda bi, i, j: (0, j)),
                  pl.BlockSpec((1, tc), lambda bi, i, j: (0, j))],
        out_specs=pl.BlockSpec((1, ts, tc), lambda bi, i, j: (bi, i, j)),
        out_shape=jax.ShapeDtypeStruct((b, s, c), BF16),
        scratch_shapes=[pltpu.VMEM((ts + 2 * CONV_HALO, tc), F32)],
        compiler_params=_cparams(("parallel", "parallel", "parallel")),
        name="conv",
    )(xbc, xbc, xbc, w8, bias)


def _ssd_body(reverse, xs_ref, b_ref, c_ref, dt_ref, dtb_ref, alog_ref, e_ref, y_ref, state_ref):
    @pl.when(pl.program_id(1) == 0)
    def _():
        state_ref[...] = jnp.zeros_like(state_ref)

    n_sub = xs_ref.shape[1] // CHUNK
    for cc in (range(n_sub - 1, -1, -1) if reverse else range(n_sub)):
        _ssd_chunk(reverse, slice(cc * CHUNK, (cc + 1) * CHUNK),
                   xs_ref, b_ref, c_ref, dt_ref, dtb_ref, alog_ref, e_ref, y_ref, state_ref)


def _ssd_chunk(reverse, rs, xs_ref, b_ref, c_ref, dt_ref, dtb_ref, alog_ref, e_ref, y_ref, state_ref):
    q = CHUNK
    lane0 = SSD_HEADS if reverse else 0
    rows = lax.broadcasted_iota(jnp.int32, (q, q), 0)
    cols = lax.broadcasted_iota(jnp.int32, (q, q), 1)
    mask = (rows <= cols) if reverse else (rows >= cols)
    tri = mask.astype(BF16)

    x = dt_ref[0, rs, :] + dtb_ref[...]
    dt = jnp.maximum(x, 0.0) + jnp.log(1.0 + jnp.exp(-jnp.abs(x)))
    da = dt * (-jnp.exp(alog_ref[...]))
    cum = _dot_split_lhs_rhs(tri, da)
    cum_t = cum.T
    dt_t = dt.T
    total = cum[0:1, :] if reverse else cum[q - 1:q, :]
    w = dt * jnp.exp(total - cum)
    w_exp = _dot(w.astype(BF16), e_ref[...])
    xs = xs_ref[0, rs, :]
    x_dec = (xs.astype(F32) * w_exp).astype(BF16)

    bm = b_ref[0, rs, :].astype(F32)
    cm = c_ref[0, rs, :]
    cm_f = cm.astype(F32)
    state_old = state_ref[...].astype(BF16)
    gw = D_STATE
    hp = SSD_HEADS // SSD_GROUPS * SSD_HEADDIM
    new_states = []
    for g in range(SSD_GROUPS):
        bt = bm[:, g * gw:(g + 1) * gw].T.astype(BF16)
        new_states.append(_dot(bt, x_dec[:, g * hp:(g + 1) * hp]))
        cb = _dot(cm[:, g * gw:(g + 1) * gw], bt)
        c_g = cm_f[:, g * gw:(g + 1) * gw]
        for jj in range(hp // LANES):
            j = g * (hp // LANES) + jj
            sl = slice(j * LANES, (j + 1) * LANES)
            rhs = jnp.concatenate([xs[:, sl], state_old[:, sl]], axis=0)
            ys = []
            for hh in range(2):
                ln = lane0 + 2 * j + hh
                col = jnp.broadcast_to(cum[:, ln:ln + 1], (q, q))
                seg = col - cum_t[ln:ln + 1, :]
                decay = jnp.exp(jnp.where(mask, seg, -jnp.inf))
                m = cb * decay * dt_t[ln:ln + 1, :]
                cs = c_g * jnp.exp(col)
                lhs = jnp.concatenate([m.astype(BF16), cs.astype(BF16)], axis=1)
                ys.append(_dot(lhs, rhs))
            lane = lax.broadcasted_iota(jnp.int32, (q, LANES), 1)
            y_ref[0, rs, sl] = jnp.where(lane < SSD_HEADDIM, ys[0], ys[1]).astype(BF16)

    dec = jnp.broadcast_to(jnp.exp(total), (8, LANES))
    dec_exp = _dot_split_lhs(dec, e_ref[...])[0:1, :]
    state_ref[...] = state_ref[...] * dec_exp + jnp.concatenate(new_states, axis=1)


def _dot_split_lhs_rhs(tri_bf16, da_f32):
    h, m, l = _split3(da_f32)
    return _dot(tri_bf16, h) + _dot(tri_bf16, m) + _dot(tri_bf16, l)


SSD_SUB = 4


def _ssd(conv_out, dt, dt_bias, a_log, expand, reverse):
    b, s, _ = conv_out.shape
    rows = SSD_SUB * CHUNK
    nc = s // rows
    cidx = (lambda c: nc - 1 - c) if reverse else (lambda c: c)
    bcol = D_INNER // (SSD_GROUPS * D_STATE)
    gn = SSD_GROUPS * D_STATE
    const = lambda w: pl.BlockSpec(w.shape, lambda bi, c: (0, 0))
    return pl.pallas_call(
        functools.partial(_ssd_body, reverse),
        grid=(b, nc),
        in_specs=[pl.BlockSpec((1, rows, D_INNER), lambda bi, c: (bi, cidx(c), 0)),
                  pl.BlockSpec((1, rows, gn), lambda bi, c: (bi, cidx(c), bcol)),
                  pl.BlockSpec((1, rows, gn), lambda bi, c: (bi, cidx(c), bcol + 1)),
                  pl.BlockSpec((1, rows, LANES), lambda bi, c: (bi, cidx(c), 0)),
                  const(dt_bias), const(a_log), const(expand)],
        out_specs=pl.BlockSpec((1, rows, D_INNER), lambda bi, c: (bi, cidx(c), 0)),
        out_shape=jax.ShapeDtypeStruct((b, s, D_INNER), BF16),
        scratch_shapes=[pltpu.VMEM((D_STATE, D_INNER), F32)],
        compiler_params=_cparams(("parallel", "arbitrary")),
        name="ssd_bwd" if reverse else "ssd_fwd",
    )(conv_out, conv_out, conv_out, dt, dt_bias, a_log, expand)


def _mix_body(x_ref, o_ref, yf_ref, yb_ref, xs_ref, z_ref, g_ref,
              woa_ref, wos_ref, wout_ref, dsk_ref, nw_ref, lg_ref, lb_ref, x1_ref, x1p_ref):
    attn = _dot(o_ref[...], woa_ref[...])
    y = yf_ref[...].astype(F32) + yb_ref[...].astype(F32) + dsk_ref[...] * xs_ref[...].astype(F32)
    y = y * z_ref[...].astype(F32)
    yn = y * lax.rsqrt(jnp.mean(y * y, axis=-1, keepdims=True) + RMS_EPS) * nw_ref[...]
    ssd = _dot(yn.astype(BF16), wos_ref[...])
    g = g_ref[...].astype(F32)
    mixed_in = g[:, :D_MODEL] * attn + g[:, D_MODEL:] * ssd
    mixed = _dot(mixed_in.astype(BF16), wout_ref[...])
    x1 = _layer_norm_rows(DN_ALPHA * x_ref[...] + mixed, lg_ref[...], lb_ref[...])
    x1_ref[...] = x1
    x1p_ref[...] = _pack_rows(x1)


def _mix(x2d, o, yf, yb, conv_out, z, g, woa, wos, wout, dsk, nw, lg, lb, tm=256):
    t = x2d.shape[0]
    const = lambda w: pl.BlockSpec(w.shape, lambda i: (0, 0))
    row = lambda n: pl.BlockSpec((tm, n), lambda i: (i, 0))
    return pl.pallas_call(
        _mix_body,
        grid=(t // tm,),
        in_specs=[row(D_MODEL), row(D_MODEL), row(D_INNER), row(D_INNER), row(D_INNER), row(D_INNER),
                  row(2 * D_MODEL), const(woa), const(wos), const(wout), const(dsk), const(nw),
                  const(lg), const(lb)],
        out_specs=[row(D_MODEL), row(PACKED_W)],
        out_shape=[jax.ShapeDtypeStruct((t, D_MODEL), F32), jax.ShapeDtypeStruct((t, PACKED_W), jnp.uint32)],
        compiler_params=_cparams(("parallel",)),
        name="mix_ln1",
    )(x2d, o, yf, yb, conv_out, z, g, woa, wos, wout, dsk, nw, lg, lb)


def _router_body(x_ref, wh_ref, wl_ref, bias_ref, su_ref, idx_ref, wts_ref, pos_ref, cnt_ref, carry_ref):
    @pl.when(pl.program_id(0) == 0)
    def _():
        carry_ref[...] = jnp.zeros_like(carry_ref)

    tm = x_ref.shape[0]
    x = x_ref[...]
    xh = x.astype(BF16)
    xl = (x - xh.astype(F32)).astype(BF16)
    wh = wh_ref[...]
    logits = _dot_nt(wh, xh) + _dot_nt(wh, xl) + _dot_nt(wl_ref[...], xh)
    scores = jax.nn.sigmoid(logits)
    choice = scores + bias_ref[...]
    gsz = N_EXPERTS // N_EXPERT_GROUPS
    shp = (N_EXPERT_GROUPS, gsz, tm)
    ch = choice.reshape(shp)
    sc = scores.reshape(shp)
    jio = lax.broadcasted_iota(jnp.int32, shp, 1)
    gio = lax.broadcasted_iota(jnp.int32, shp, 0)
    eio = gio * gsz + jio
    neg = -jnp.inf

    m1 = jnp.max(ch, axis=1, keepdims=True)
    i1 = jnp.min(jnp.where(ch == m1, jio, gsz), axis=1, keepdims=True)
    m2 = jnp.max(jnp.where(jio == i1, neg, ch), axis=1, keepdims=True)
    gs = m1 + m2
    gio1 = lax.broadcasted_iota(jnp.int32, (N_EXPERT_GROUPS, 1, tm), 0)
    sel = jnp.zeros(gs.shape, jnp.bool_)
    cur = gs
    for _ in range(TOPK_GROUPS):
        m = jnp.max(cur, axis=0, keepdims=True)
        gi = jnp.min(jnp.where(cur == m, gio1, N_EXPERT_GROUPS), axis=0, keepdims=True)
        hit = gio1 == gi
        sel = jnp.logical_or(sel, hit)
        cur = jnp.where(hit, neg, cur)
    masked = jnp.where(sel, ch, neg)

    def red(op, v):
        return op(op(v, axis=1, keepdims=True), axis=0, keepdims=True)

    hits, idxs, ws = [], [], []
    for _ in range(TOP_K):
        m = red(jnp.max, masked)
        ei = red(jnp.min, jnp.where(masked == m, eio, N_EXPERTS))
        hit = eio == ei
        hits.append(hit)
        idxs.append(ei)
        ws.append(red(jnp.sum, jnp.where(hit, sc, 0.0)))
        masked = jnp.where(hit, neg, masked)
    wsum = ws[0]
    for k in range(1, TOP_K):
        wsum = wsum + ws[k]

    onehot = hits[0]
    for k in range(1, TOP_K):
        onehot = jnp.logical_or(onehot, hits[k])
    oh = onehot.astype(F32).reshape(N_EXPERTS, tm)
    before = _dot(oh.astype(BF16), su_ref[...]) + carry_ref[...]
    before3 = before.reshape(shp)
    for k in range(TOP_K):
        idx_ref[k:k + 1, :] = idxs[k].reshape(1, tm)
        wts_ref[k:k + 1, :] = (ws[k] / wsum * ROUTED_SCALE).reshape(1, tm)
        pos_ref[k:k + 1, :] = red(jnp.sum, jnp.where(hits[k], before3, 0.0)).reshape(1, tm).astype(jnp.int32)
    carry_ref[...] = carry_ref[...] + jnp.sum(oh, axis=1, keepdims=True)
    cnt_ref[...] = jnp.broadcast_to(carry_ref[...], cnt_ref.shape)


def _router(x1, wr_hi, wr_lo, bias, su, tm=512):
    t = x1.shape[0]
    const = lambda w: pl.BlockSpec(w.shape, lambda i: (0, 0))
    out_tok = pl.BlockSpec((TOP_K, tm), lambda i: (0, i))
    return pl.pallas_call(
        _router_body,
        grid=(t // tm,),
        in_specs=[pl.BlockSpec((tm, D_MODEL), lambda i: (i, 0)), const(wr_hi), const(wr_lo), const(bias),
                  const(su)],
        out_specs=[out_tok, out_tok, out_tok, pl.BlockSpec((N_EXPERTS, LANES), lambda i: (0, 0))],
        out_shape=[jax.ShapeDtypeStruct((TOP_K, t), jnp.int32),
                   jax.ShapeDtypeStruct((TOP_K, t), F32),
                   jax.ShapeDtypeStruct((TOP_K, t), jnp.int32),
                   jax.ShapeDtypeStruct((N_EXPERTS, LANES), F32)],
        scratch_shapes=[pltpu.VMEM((N_EXPERTS, 1), F32)],
        compiler_params=_cparams(("arbitrary",)),
        name="router",
    )(x1, wr_hi, wr_lo, bias, su)


def _expert_body(be_ref, x_ref, wg_ref, wu_ref, wd_ref, y_ref):
    hi, lo = _unpack_rows(x_ref[...])
    xb = jnp.concatenate([hi.astype(BF16), lo.astype(BF16)], axis=1)
    h = _silu(_dot(xb, wg_ref[0])) * _dot(xb, wu_ref[0])
    y_ref[...] = _pack_rows(_dot(h.astype(BF16), wd_ref[0]))


def _experts(block_e, xs, wg, wu, wd):
    rows = xs.shape[0]
    nb = rows // EXPERT_BLOCK
    grid_spec = pltpu.PrefetchScalarGridSpec(
        num_scalar_prefetch=1,
        grid=(nb,),
        in_specs=[pl.BlockSpec((EXPERT_BLOCK, PACKED_W), lambda b, be: (b, 0)),
                  pl.BlockSpec((1, D_MODEL, D_EXPERT), lambda b, be: (be[b], 0, 0)),
                  pl.BlockSpec((1, D_MODEL, D_EXPERT), lambda b, be: (be[b], 0, 0)),
                  pl.BlockSpec((1, D_EXPERT, D_MODEL), lambda b, be: (be[b], 0, 0))],
        out_specs=pl.BlockSpec((EXPERT_BLOCK, PACKED_W), lambda b, be: (b, 0)),
    )
    return pl.pallas_call(
        _expert_body,
        grid_spec=grid_spec,
        out_shape=jax.ShapeDtypeStruct((rows, PACKED_W), jnp.uint32),
        compiler_params=_cparams(("parallel",)),
        name="moe_experts",
    )(block_e, xs, wg, wu, wd)


def _final_body(x1_ref, wt_ref, g_ref, wsg_ref, wsu_ref, wsd_ref, lg_ref, lb_ref, out_ref):
    x1 = x1_ref[...]
    xb = x1.astype(BF16)
    h = _silu(_dot(xb, wsg_ref[...])) * _dot(xb, wsu_ref[...])
    acc = DN_ALPHA * x1 + _dot(h.astype(BF16), wsd_ref[...])
    wt = wt_ref[...]
    acc_hi, acc_lo = acc[:, :PACKED_W], acc[:, PACKED_W:]
    for k in range(TOP_K):
        hi, lo = _unpack_rows(g_ref[k])
        acc_hi = acc_hi + wt[:, k:k + 1] * hi
        acc_lo = acc_lo + wt[:, k:k + 1] * lo
    acc = jnp.concatenate([acc_hi, acc_lo], axis=1)
    out_ref[...] = _layer_norm_rows(acc, lg_ref[...], lb_ref[...])


def _final(x1, wt_rows, g, wsg, wsu, wsd, lg, lb, tm=256):
    t = x1.shape[0]
    const = lambda w: pl.BlockSpec(w.shape, lambda i: (0, 0))
    return pl.pallas_call(
        _final_body,
        grid=(t // tm,),
        in_specs=[pl.BlockSpec((tm, D_MODEL), lambda i: (i, 0)),
                  pl.BlockSpec((tm, TOP_K), lambda i: (i, 0)),
                  pl.BlockSpec((TOP_K, tm, PACKED_W), lambda i: (0, i, 0)),
                  const(wsg), const(wsu), const(wsd), const(lg), const(lb)],
        out_specs=pl.BlockSpec((tm, D_MODEL), lambda i: (i, 0)),
        out_shape=jax.ShapeDtypeStruct((t, D_MODEL), F32),
        compiler_params=_cparams(("parallel",)),
        name="moe_combine_ln2",
    )(x1, wt_rows, g, wsg, wsu, wsd, lg, lb)


SC_CORES = 2
SC_SUBCORES = 16
SC_CHUNK = 128


def _sc_scatter_rows(src, dest, pad_rows, rows):
    t, w = src.shape
    workers = SC_CORES * SC_SUBCORES
    per_w = t // workers
    steps = per_w // SC_CHUNK
    pad_steps = pad_rows.shape[0] // (workers * SC_CHUNK)
    assert steps * SC_CHUNK * workers == t and pad_steps * SC_CHUNK * workers == pad_rows.shape[0]
    mesh = plsc.VectorSubcoreMesh(core_axis_name="c", subcore_axis_name="s")
    idx_tok = dest.reshape(TOP_K, workers, steps, SC_CHUNK).transpose(1, 2, 0, 3)
    idx_pad = pad_rows.reshape(workers, pad_steps, SC_CHUNK)
    zeros = jnp.zeros((SC_CHUNK, w), src.dtype)

    def body(src_hbm, tok_hbm, pad_hbm, zero_hbm, out_hbm, tok_v, pad_v, rows_v, sem):
        wid = lax.axis_index("s") * SC_CORES + lax.axis_index("c")
        pltpu.sync_copy(tok_hbm.at[wid], tok_v)
        pltpu.sync_copy(pad_hbm.at[wid], pad_v)

        @pl.loop(0, steps)
        def _(i):
            off = pl.multiple_of(wid * per_w + i * SC_CHUNK, SC_CHUNK)
            pltpu.sync_copy(src_hbm.at[pl.ds(off, SC_CHUNK)], rows_v)
            copies = [pltpu.async_copy(rows_v, out_hbm.at[tok_v.at[i, k]], sem) for k in range(TOP_K)]
            for c in copies:
                c.wait()

        pltpu.sync_copy(zero_hbm, rows_v)

        @pl.loop(0, pad_steps)
        def _(i):
            pltpu.async_copy(rows_v, out_hbm.at[pad_v.at[i]], sem).wait()

    return pl.kernel(
        body, mesh=mesh,
        out_type=jax.ShapeDtypeStruct((rows, w), src.dtype),
        scratch_types=[pltpu.VMEM((steps, TOP_K, SC_CHUNK), jnp.int32), pltpu.VMEM((pad_steps, SC_CHUNK), jnp.int32),
                       pltpu.VMEM((SC_CHUNK, w), src.dtype), pltpu.SemaphoreType.DMA],
    )(src, idx_tok, idx_pad, zeros)


def _sc_gather_rows(table, idx):
    n = idx.shape[0]
    w = table.shape[1]
    workers = SC_CORES * SC_SUBCORES
    per_w = n // workers
    steps = per_w // SC_CHUNK
    assert steps * SC_CHUNK * workers == n
    mesh = plsc.VectorSubcoreMesh(core_axis_name="c", subcore_axis_name="s")

    def body(table_hbm, idx_hbm, out_hbm, idx_v, rows_v, sem):
        wid = lax.axis_index("s") * SC_CORES + lax.axis_index("c")
        pltpu.sync_copy(idx_hbm.at[wid], idx_v)

        @pl.loop(0, steps)
        def _(i):
            off = pl.multiple_of(wid * per_w + i * SC_CHUNK, SC_CHUNK)
            pltpu.async_copy(table_hbm.at[idx_v.at[i]], rows_v, sem).wait()
            pltpu.sync_copy(rows_v, out_hbm.at[pl.ds(off, SC_CHUNK)])

    return pl.kernel(
        body, mesh=mesh,
        out_type=jax.ShapeDtypeStruct((n, w), table.dtype),
        scratch_types=[pltpu.VMEM((steps, SC_CHUNK), jnp.int32), pltpu.VMEM((SC_CHUNK, w), table.dtype),
                       pltpu.SemaphoreType.DMA],
    )(table, idx.reshape(workers, steps, SC_CHUNK))


def _dest_body(ps_ref, idx_ref, pos_ref, out_ref):
    idx = idx_ref[...]
    acc = pos_ref[...]
    for e in range(N_EXPERTS):
        acc = acc + jnp.where(idx == e, ps_ref[e], 0)
    out_ref[...] = acc


def _dest_rows(pad_start, idx, pos, tn=2048):
    t = idx.shape[1]
    tn = min(tn, t)
    tok = pl.BlockSpec((TOP_K, tn), lambda i, ps: (0, i))
    grid_spec = pltpu.PrefetchScalarGridSpec(
        num_scalar_prefetch=1, grid=(t // tn,), in_specs=[tok, tok], out_specs=tok)
    return pl.pallas_call(
        _dest_body,
        grid_spec=grid_spec,
        out_shape=jax.ShapeDtypeStruct((TOP_K, t), jnp.int32),
        compiler_params=_cparams(("parallel",)),
        name="moe_dest",
    )(pad_start, idx, pos)


def _rot_cols(w):
    half = w.shape[-1] // 2
    return jnp.concatenate([-w[..., half:], w[..., :half]], axis=-1)


def _prepare(p, seq_lens):
    f = {}
    w_in = p["w_in"]
    o = np.cumsum((0, Q_LORA, KV_LORA, QK_ROPE, D_INNER, CONV_DIM, SSD_HEADS, SSD_HEADS, D_MODEL, D_MODEL))
    seg = lambda i: w_in[:, o[i]:o[i + 1]]
    zeros = lambda n: jnp.zeros((D_MODEL, n), F32)
    f["wq"] = jnp.concatenate([seg(0), seg(1), seg(2), _rot_cols(seg(2)), zeros(QKV_W - KROPE_OFF - 2 * QK_ROPE)],
                              axis=1).astype(BF16)
    f["wz"] = seg(3).astype(BF16)
    f["wx"] = seg(4).astype(BF16)
    f["wdt"] = jnp.concatenate([seg(5), seg(6), zeros(LANES - 2 * SSD_HEADS)], axis=1).astype(BF16)
    f["wg"] = jnp.concatenate([seg(7), seg(8)], axis=1).astype(BF16)

    f["qn"] = p["q_norm_w"].reshape(1, Q_LORA)
    f["kvn"] = p["kv_norm_w"].reshape(1, KV_LORA)
    w_uq = p["w_uq"]
    nope, rope = w_uq[..., :QK_NOPE], w_uq[..., QK_NOPE:]
    padq = HEAD_PAD - QK_NOPE - QK_ROPE
    zq = lambda n: jnp.zeros((Q_LORA, N_HEADS, n), F32)
    wq_a = jnp.concatenate([nope, rope, zq(padq)], axis=-1).reshape(Q_LORA, N_HEADS * HEAD_PAD)
    wq_b = jnp.concatenate([zq(QK_NOPE), _rot_cols(rope), zq(padq)], axis=-1).reshape(Q_LORA, N_HEADS * HEAD_PAD)
    f["w_q"] = jnp.concatenate([wq_a, wq_b], axis=1).astype(BF16)
    w_ukv = p["w_ukv"]
    wk_t = jnp.transpose(w_ukv[..., :QK_NOPE], (1, 2, 0))
    eye = jnp.eye(QK_ROPE, dtype=F32)
    rope_rows = jnp.concatenate([jnp.zeros((QK_ROPE, KV_LORA), F32), eye, eye,
                                 jnp.zeros((QK_ROPE, LANES - 2 * QK_ROPE), F32)], axis=1)
    wk = jnp.concatenate([
        jnp.concatenate([wk_t, jnp.zeros((N_HEADS, QK_NOPE, LANES), F32)], axis=2),
        jnp.broadcast_to(rope_rows, (N_HEADS, QK_ROPE, KV_LORA + LANES)),
        jnp.zeros((N_HEADS, padq, KV_LORA + LANES), F32)], axis=1)
    f["w_k"] = wk.reshape(N_HEADS * HEAD_PAD, KV_LORA + LANES).astype(BF16)
    w_v = jnp.concatenate([w_ukv[..., QK_NOPE:], jnp.zeros((KV_LORA, N_HEADS, HEAD_PAD - V_DIM), F32)], axis=-1)
    f["w_v"] = w_v.reshape(KV_LORA, N_HEADS * HEAD_PAD).astype(BF16)
    f["v_one"] = (jnp.arange(N_HEADS * HEAD_PAD) % HEAD_PAD == V_DIM).astype(F32).reshape(1, -1)

    scale = (QK_NOPE + QK_ROPE) ** -0.5 * math.log2(math.e)
    f["rope"] = {}
    for s in sorted(set(seq_lens)):
        inv_freq = ROPE_THETA ** (-jnp.arange(0, QK_ROPE, 2, dtype=F32) / QK_ROPE)
        ang = jnp.arange(s, dtype=F32)[:, None] * inv_freq[None, :]
        cos, sin = jnp.cos(ang), jnp.sin(ang)
        one, zero = jnp.ones((s, QK_NOPE), F32), jnp.zeros((s, QK_NOPE), F32)
        zp = jnp.zeros((s, padq), F32)
        cq_t = scale * jnp.concatenate([one, cos, cos, zp], axis=1)
        sq_t = scale * jnp.concatenate([zero, sin, sin, zp], axis=1)
        tk_t = jnp.concatenate([cos, cos, sin, sin, jnp.zeros((s, LANES - 2 * QK_ROPE), F32)], axis=1)
        f["rope"][s] = (cq_t, sq_t, tk_t)

    f["conv_w"] = jnp.concatenate([p["conv_w"], jnp.zeros((8 - CONV_WIDTH, CONV_DIM), F32)], axis=0)
    f["conv_b"] = p["conv_b"].reshape(1, CONV_DIM)
    pad_h = jnp.zeros((LANES - 2 * SSD_HEADS,), F32)
    f["dt_bias"] = jnp.concatenate([p["dt_bias_f"], p["dt_bias_b"], pad_h]).reshape(1, LANES)
    f["a_log"] = jnp.concatenate([p["a_log_f"], p["a_log_b"], pad_h]).reshape(1, LANES)
    head_of_lane = jnp.arange(D_INNER, dtype=jnp.int32) // SSD_HEADDIM
    lane_id = jnp.arange(LANES, dtype=jnp.int32)[:, None]
    f["expand_f"] = (lane_id == head_of_lane[None, :]).astype(BF16)
    f["expand_b"] = (lane_id == head_of_lane[None, :] + SSD_HEADS).astype(BF16)
    f["d_skip"] = jnp.repeat(p["d_skip"], SSD_HEADDIM).reshape(1, D_INNER)
    f["ssd_nw"] = p["ssd_norm_w"].reshape(1, D_INNER)
    f["w_oa"] = p["w_o_attn"].astype(BF16)
    f["w_os"] = p["w_o_ssd"].astype(BF16)
    f["w_out"] = p["w_out"].astype(BF16)
    f["ln1_g"] = p["ln1_g"].reshape(1, D_MODEL)
    f["ln1_b"] = p["ln1_b"].reshape(1, D_MODEL)

    wr_t = p["w_router"].T
    wr_hi = wr_t.astype(BF16)
    f["wr_hi"] = wr_hi
    f["wr_lo"] = (wr_t - wr_hi.astype(F32)).astype(BF16)
    f["r_bias"] = p["router_bias"].reshape(N_EXPERTS, 1)
    f["w_gate"] = p["w_gate"].astype(BF16)
    f["w_up"] = p["w_up"].astype(BF16)
    f["w_down"] = p["w_down"].astype(BF16)
    f["ws_gate"] = p["ws_gate"].astype(BF16)
    f["ws_up"] = p["ws_up"].astype(BF16)
    f["ws_down"] = p["ws_down"].astype(BF16)
    f["ln2_g"] = p["ln2_g"].reshape(1, D_MODEL)
    f["ln2_b"] = p["ln2_b"].reshape(1, D_MODEL)
    return f


ROUTER_TILE = 512


def _layer(x, f):
    b, s, d = x.shape
    t = b * s
    x2d = x.reshape(t, d)
    qkv, z, xbc, dt, g = _in_proj(x2d, f["wq"], f["wz"], f["wx"], f["wdt"], f["wg"])

    cq_t, sq_t, tk_t = f["rope"][s]
    q, kt, v = _mla_prep(qkv.reshape(b, s, QKV_W), f["qn"], f["kvn"], f["w_q"], f["w_k"], f["w_v"], f["v_one"],
                         cq_t, sq_t, tk_t)
    o = _attention(q, kt, v)

    conv_out = _conv(xbc.reshape(b, s, CONV_DIM), f["conv_w"], f["conv_b"])
    dt3 = dt.reshape(b, s, LANES)
    y_f = _ssd(conv_out, dt3, f["dt_bias"], f["a_log"], f["expand_f"], reverse=False)
    y_b = _ssd(conv_out, dt3, f["dt_bias"], f["a_log"], f["expand_b"], reverse=True)

    x1, x1p = _mix(x2d, o.reshape(t, -1), y_f.reshape(t, -1), y_b.reshape(t, -1), conv_out.reshape(t, -1), z, g,
              f["w_oa"], f["w_os"], f["w_out"], f["d_skip"], f["ssd_nw"], f["ln1_g"], f["ln1_b"])

    su = (jnp.arange(ROUTER_TILE)[:, None] < jnp.arange(ROUTER_TILE)[None, :]).astype(BF16)
    idx, wts, pos, cnt = _router(x1, f["wr_hi"], f["wr_lo"], f["r_bias"], su, tm=ROUTER_TILE)

    counts = cnt[:, 0].astype(jnp.int32)
    padded = (counts + EXPERT_BLOCK - 1) // EXPERT_BLOCK * EXPERT_BLOCK
    pad_end = jnp.cumsum(padded)
    pad_start = pad_end - padded
    nb = t * TOP_K // EXPERT_BLOCK + N_EXPERTS
    n_used = pad_end[-1] // EXPERT_BLOCK
    blk_row = jnp.minimum(jnp.arange(nb, dtype=jnp.int32), n_used - 1) * EXPERT_BLOCK
    block_e = jnp.minimum(jnp.sum((blk_row[:, None] >= pad_end[None, :]).astype(jnp.int32), axis=1), N_EXPERTS - 1)

    seg_len = jnp.concatenate([padded - counts, (nb * EXPERT_BLOCK - pad_end[-1]).reshape(1)])
    seg_first = jnp.concatenate([pad_start + counts, pad_end[-1:]])
    seg_end = jnp.cumsum(seg_len)
    j = jnp.arange(N_EXPERTS * EXPERT_BLOCK, dtype=jnp.int32)
    onehot = (jnp.sum((j[:, None] >= seg_end[None, :]).astype(jnp.int32), axis=1)[:, None]
              == jnp.arange(N_EXPERTS + 1, dtype=jnp.int32)[None, :])
    pad_rows = j + jnp.sum(jnp.where(onehot, (seg_first - (seg_end - seg_len))[None, :], 0), axis=1)

    dest = _dest_rows(pad_start, idx, pos)
    xs = _sc_scatter_rows(x1p, dest, pad_rows.astype(jnp.int32), nb * EXPERT_BLOCK)
    ys = _experts(block_e, xs, f["w_gate"], f["w_up"], f["w_down"])
    g = _sc_gather_rows(ys, dest.reshape(TOP_K * t)).reshape(TOP_K, t, PACKED_W)
    y = _final(x1, wts.T, g, f["ws_gate"], f["ws_up"], f["ws_down"], f["ln2_g"], f["ln2_b"])
    return y.reshape(b, s, d)


def kernel(x_prompt, x_sample, w_in, q_norm_w, kv_norm_w, w_uq, w_ukv, w_o_attn, conv_w, conv_b, dt_bias_f,
           dt_bias_b, a_log_f, a_log_b, d_skip, ssd_norm_w, w_o_ssd, w_out, ln1_g, ln1_b, w_router, router_bias,
           w_gate, w_up, w_down, ws_gate, ws_up, ws_down, ln2_g, ln2_b):
    params = dict(w_in=w_in, q_norm_w=q_norm_w, kv_norm_w=kv_norm_w, w_uq=w_uq, w_ukv=w_ukv, w_o_attn=w_o_attn,
                  conv_w=conv_w, conv_b=conv_b, dt_bias_f=dt_bias_f, dt_bias_b=dt_bias_b, a_log_f=a_log_f,
                  a_log_b=a_log_b, d_skip=d_skip, ssd_norm_w=ssd_norm_w, w_o_ssd=w_o_ssd, w_out=w_out,
                  ln1_g=ln1_g, ln1_b=ln1_b, w_router=w_router, router_bias=router_bias, w_gate=w_gate,
                  w_up=w_up, w_down=w_down, ws_gate=ws_gate, ws_up=ws_up, ws_down=ws_down, ln2_g=ln2_g,
                  ln2_b=ln2_b)
    assert w_in.shape[0] == DEPTH == 1
    params = {k: v[0] for k, v in params.items()}
    f = _prepare(params, (x_prompt.shape[1], x_sample.shape[1]))
    return (_layer(x_prompt, f), _layer(x_sample, f))
```

```python
import functools
import math

import jax
import jax.numpy as jnp
import numpy as np
from jax import lax
from jax.experimental import pallas as pl
from jax.experimental.pallas import tpu as pltpu
from jax.experimental.pallas import tpu_sc as plsc

F32 = jnp.float32
BF16 = jnp.bfloat16

D_MODEL = 1024
DEPTH = 1
N_HEADS = 16
QK_NOPE = 64
QK_ROPE = 32
V_DIM = 64
Q_LORA = 384
KV_LORA = 256
ROPE_THETA = 10000.0
D_INNER = 2 * D_MODEL
SSD_HEADDIM = 64
SSD_HEADS = D_INNER // SSD_HEADDIM
SSD_GROUPS = 4
D_STATE = 128
CONV_WIDTH = 5
CONV_DIM = D_INNER + 2 * SSD_GROUPS * D_STATE
CHUNK = 128
N_EXPERTS = 64
TOP_K = 8
N_EXPERT_GROUPS = 8
TOPK_GROUPS = 4
D_EXPERT = D_MODEL // 4
D_SHARED = D_MODEL // 4
ROUTED_SCALE = 2.5
LN_EPS = 1e-5
RMS_EPS = 1e-6
DN_ALPHA = (2 * DEPTH) ** 0.25

LANES = 128
HEAD_PAD = 128
QKV_W = 768
KROPE_OFF = Q_LORA + KV_LORA
EXPERT_BLOCK = 512
PACKED_W = D_MODEL // 2
VMEM_LIMIT = 56 * 1024 * 1024


def _cparams(sem, vmem=VMEM_LIMIT):
    return pltpu.CompilerParams(dimension_semantics=sem, vmem_limit_bytes=vmem)


def _dot(a, b):
    return jnp.dot(a, b, preferred_element_type=F32)


def _dot_nt(a, b):
    return lax.dot_general(a, b, (((1,), (1,)), ((), ())), preferred_element_type=F32)


def _split3(a):
    h = a.astype(BF16)
    r = a - h.astype(F32)
    m = r.astype(BF16)
    l = (r - m.astype(F32)).astype(BF16)
    return h, m, l


def _dot_split_lhs(a_f32, b_bf16):
    h, m, l = _split3(a_f32)
    return _dot(h, b_bf16) + _dot(m, b_bf16) + _dot(l, b_bf16)


def _silu(x):
    return x * jax.nn.sigmoid(x)


def _pack_rows(x):
    n = x.shape[1] // 2
    hi = lax.bitcast_convert_type(x[:, :n].astype(BF16).astype(F32), jnp.uint32)
    lo = lax.bitcast_convert_type(x[:, n:].astype(BF16).astype(F32), jnp.uint32)
    return hi | (lo >> 16)


def _unpack_rows(w):
    hi = lax.bitcast_convert_type(w & jnp.uint32(0xFFFF0000), F32)
    lo = lax.bitcast_convert_type(w << 16, F32)
    return hi, lo


def _layer_norm_rows(r, g, b):
    mu = jnp.mean(r, axis=-1, keepdims=True)
    d = r - mu
    var = jnp.mean(d * d, axis=-1, keepdims=True)
    return d * lax.rsqrt(var + LN_EPS) * g + b


def _inproj_body(x_ref, wq_ref, wz_ref, wx_ref, wdt_ref, wg_ref,
                 qkv_ref, z_ref, xbc_ref, dt_ref, g_ref):
    xb = x_ref[...].astype(BF16)
    qkv_ref[...] = _dot(xb, wq_ref[...]).astype(BF16)
    z_ref[...] = _silu(_dot(xb, wz_ref[...])).astype(BF16)
    xbc_ref[...] = _dot(xb, wx_ref[...]).astype(BF16)
    dt_ref[...] = _dot(xb, wdt_ref[...])
    g_ref[...] = jax.nn.sigmoid(_dot(xb, wg_ref[...])).astype(BF16)


def _in_proj(x2d, wq, wz, wx, wdt, wg, tm=512):
    t = x2d.shape[0]
    const = lambda w: pl.BlockSpec(w.shape, lambda i: (0, 0), pipeline_mode=pl.Buffered(1))
    row = lambda n: pl.BlockSpec((tm, n), lambda i: (i, 0))
    return pl.pallas_call(
        _inproj_body,
        grid=(t // tm,),
        in_specs=[row(D_MODEL), const(wq), const(wz), const(wx), const(wdt), const(wg)],
        out_specs=[row(QKV_W), row(D_INNER), row(CONV_DIM), row(LANES), row(2 * D_MODEL)],
        out_shape=[jax.ShapeDtypeStruct((t, QKV_W), BF16),
                   jax.ShapeDtypeStruct((t, D_INNER), BF16),
                   jax.ShapeDtypeStruct((t, CONV_DIM), BF16),
                   jax.ShapeDtypeStruct((t, LANES), F32),
                   jax.ShapeDtypeStruct((t, 2 * D_MODEL), BF16)],
        compiler_params=_cparams(("parallel",)),
        name="in_proj",
    )(x2d, wq, wz, wx, wdt, wg)


def _mla_prep_body(qkv_ref, qn_ref, kvn_ref, wq_ref, wk_ref, wv_ref, vone_ref, cq_ref, sq_ref, tk_ref,
                   q_ref, kt_ref, v_ref):
    qkv = qkv_ref[0].astype(F32)
    cq = qkv[:, :Q_LORA]
    cq = cq * lax.rsqrt(jnp.mean(cq * cq, axis=-1, keepdims=True) + RMS_EPS) * qn_ref[...]
    ckv = qkv[:, Q_LORA:KROPE_OFF]
    ckv = ckv * lax.rsqrt(jnp.mean(ckv * ckv, axis=-1, keepdims=True) + RMS_EPS) * kvn_ref[...]
    ckv_b = ckv.astype(BF16)
    qq = _dot(cq.astype(BF16), wq_ref[...])
    half = N_HEADS * HEAD_PAD
    cq_t = cq_ref[...]
    sq_t = sq_ref[...]
    for h in range(N_HEADS):
        lo = h * HEAD_PAD
        q_ref[0, :, lo:lo + HEAD_PAD] = (qq[:, lo:lo + HEAD_PAD] * cq_t
                                          + qq[:, half + lo:half + lo + HEAD_PAD] * sq_t).astype(BF16)
    kr = (qkv[:, KROPE_OFF:] * tk_ref[...]).astype(BF16)
    e = jnp.concatenate([ckv_b, kr], axis=1)
    kt_ref[0] = _dot_nt(wk_ref[...], e).astype(BF16)
    v_ref[0] = (_dot(ckv_b, wv_ref[...]) + vone_ref[...]).astype(BF16)


def _mla_prep(qkv, qn, kvn, wq, wk, wv, vone, cq_t, sq_t, tk_t, ts=512):
    b, s, _ = qkv.shape
    const = lambda w: pl.BlockSpec(w.shape, lambda bi, i: (0, 0))
    tab = pl.BlockSpec((ts, LANES), lambda bi, i: (i, 0))
    hw = N_HEADS * HEAD_PAD
    return pl.pallas_call(
        _mla_prep_body,
        grid=(b, s // ts),
        in_specs=[pl.BlockSpec((1, ts, QKV_W), lambda bi, i: (bi, i, 0)),
                  const(qn), const(kvn), const(wq), const(wk), const(wv), const(vone), tab, tab, tab],
        out_specs=[pl.BlockSpec((1, ts, hw), lambda bi, i: (bi, i, 0)),
                   pl.BlockSpec((1, hw, ts), lambda bi, i: (bi, 0, i)),
                   pl.BlockSpec((1, ts, hw), lambda bi, i: (bi, i, 0))],
        out_shape=[jax.ShapeDtypeStruct((b, s, hw), BF16),
                   jax.ShapeDtypeStruct((b, hw, s), BF16),
                   jax.ShapeDtypeStruct((b, s, hw), BF16)],
        compiler_params=_cparams(("parallel", "parallel")),
        name="mla_prep",
    )(qkv, qn, kvn, wq, wk, wv, vone, cq_t, sq_t, tk_t)


def _attn_body(q_ref, kt_ref, v_ref, o_ref):
    tq = q_ref.shape[1]
    for r in range(tq // ATTN_ROWS):
        rows = slice(r * ATTN_ROWS, (r + 1) * ATTN_ROWS)
        outs = []
        for hh in range(2):
            sl = slice(hh * HEAD_PAD, (hh + 1) * HEAD_PAD)
            s = _dot(q_ref[0, rows, sl], kt_ref[0, sl, :])
            m = jnp.max(s, axis=-1, keepdims=True)
            p = jnp.exp2(s - m).astype(BF16)
            o = _dot(p, v_ref[0, :, sl])
            outs.append(o / o[:, V_DIM:V_DIM + 1])
        lane = lax.broadcasted_iota(jnp.int32, outs[0].shape, 1)
        o_ref[0, rows, :] = jnp.where(lane < V_DIM, outs[0], pltpu.roll(outs[1], V_DIM, axis=1)).astype(BF16)


ATTN_ROWS = 256


def _attention(q, kt, v, tq=1024):
    b, s, _ = q.shape
    pairs = N_HEADS // 2
    return pl.pallas_call(
        _attn_body,
        grid=(b, pairs, s // tq),
        in_specs=[pl.BlockSpec((1, tq, 2 * HEAD_PAD), lambda bi, j, i: (bi, i, j)),
                  pl.BlockSpec((1, 2 * HEAD_PAD, s), lambda bi, j, i: (bi, j, 0)),
                  pl.BlockSpec((1, s, 2 * HEAD_PAD), lambda bi, j, i: (bi, 0, j))],
        out_specs=pl.BlockSpec((1, tq, 2 * V_DIM), lambda bi, j, i: (bi, i, j)),
        out_shape=jax.ShapeDtypeStruct((b, s, N_HEADS * V_DIM), BF16),
        compiler_params=_cparams(("parallel", "parallel", "arbitrary")),
        name="attention",
    )(q, kt, v)


CONV_HALO = 16


def _conv_body(prev_ref, main_ref, next_ref, w_ref, b_ref, o_ref, ext_ref):
    i = pl.program_id(1)
    n = pl.num_programs(1)
    ts = main_ref.shape[1]
    prev = prev_ref[0].astype(F32)
    nxt = next_ref[0].astype(F32)
    ext_ref[0:CONV_HALO, :] = jnp.where(i > 0, prev, 0.0)
    ext_ref[CONV_HALO:CONV_HALO + ts, :] = main_ref[0].astype(F32)
    ext_ref[CONV_HALO + ts:, :] = jnp.where(i < n - 1, nxt, 0.0)
    acc = b_ref[...]
    for k in range(CONV_WIDTH):
        off = CONV_HALO - CONV_WIDTH // 2 + k
        acc = acc + ext_ref[off:off + ts, :] * w_ref[k:k + 1, :]
    o_ref[0] = _silu(acc).astype(BF16)


def _conv(xbc, w8, bias, ts=1024, tc=1024):
    b, s, c = xbc.shape
    r = ts // CONV_HALO
    last = s // CONV_HALO - 1
    return pl.pallas_call(
        _conv_body,
        grid=(b, s // ts, c // tc),
        in_specs=[pl.BlockSpec((1, CONV_HALO, tc), lambda bi, i, j: (bi, jnp.maximum(i * r - 1, 0), j)),
                  pl.BlockSpec((1, ts, tc), lambda bi, i, j: (bi, i, j)),
                  pl.BlockSpec((1, CONV_HALO, tc), lambda bi, i, j: (bi, jnp.minimum((i + 1) * r, last), j)),
                  pl.BlockSpec((8, tc), lambda bi, i, j: (0, j)),
                  pl.BlockSpec((1, tc), lambda bi, i, j: (0, j))],
        out_specs=pl.BlockSpec((1, ts, tc), lambda bi, i, j: (bi, i, j)),
        out_shape=jax.ShapeDtypeStruct((b, s, c), BF16),
        scratch_shapes=[pltpu.VMEM((ts + 2 * CONV_HALO, tc), F32)],
        compiler_params=_cparams(("parallel", "parallel", "parallel")),
        name="conv",
    )(xbc, xbc, xbc, w8, bias)


def _ssd_body(reverse, xs_ref, b_ref, c_ref, dt_ref, dtb_ref, alog_ref, e_ref, y_ref, state_ref):
    @pl.when(pl.program_id(1) == 0)
    def _():
        state_ref[...] = jnp.zeros_like(state_ref)

    n_sub = xs_ref.shape[1] // CHUNK
    for cc in (range(n_sub - 1, -1, -1) if reverse else range(n_sub)):
        _ssd_chunk(reverse, slice(cc * CHUNK, (cc + 1) * CHUNK),
                   xs_ref, b_ref, c_ref, dt_ref, dtb_ref, alog_ref, e_ref, y_ref, state_ref)


def _ssd_chunk(reverse, rs, xs_ref, b_ref, c_ref, dt_ref, dtb_ref, alog_ref, e_ref, y_ref, state_ref):
    q = CHUNK
    lane0 = SSD_HEADS if reverse else 0
    rows = lax.broadcasted_iota(jnp.int32, (q, q), 0)
    cols = lax.broadcasted_iota(jnp.int32, (q, q), 1)
    mask = (rows <= cols) if reverse else (rows >= cols)
    tri = mask.astype(BF16)

    x = dt_ref[0, rs, :] + dtb_ref[...]
    dt = jnp.maximum(x, 0.0) + jnp.log(1.0 + jnp.exp(-jnp.abs(x)))
    da = dt * (-jnp.exp(alog_ref[...]))
    cum = _dot_split_lhs_rhs(tri, da)
    cum_t = cum.T
    dt_t = dt.T
    total = cum[0:1, :] if reverse else cum[q - 1:q, :]
    w = dt * jnp.exp(total - cum)
    w_exp = _dot(w.astype(BF16), e_ref[...])
    xs = xs_ref[0, rs, :]
    x_dec = (xs.astype(F32) * w_exp).astype(BF16)

    bm = b_ref[0, rs, :].astype(F32)
    cm = c_ref[0, rs, :]
    cm_f = cm.astype(F32)
    state_old = state_ref[...].astype(BF16)
    gw = D_STATE
    hp = SSD_HEADS // SSD_GROUPS * SSD_HEADDIM
    new_states = []
    for g in range(SSD_GROUPS):
        bt = bm[:, g * gw:(g + 1) * gw].T.astype(BF16)
        new_states.append(_dot(bt, x_dec[:, g * hp:(g + 1) * hp]))
        cb = _dot(cm[:, g * gw:(g + 1) * gw], bt)
        c_g = cm_f[:, g * gw:(g + 1) * gw]
        for jj in range(hp // LANES):
            j = g * (hp // LANES) + jj
            sl = slice(j * LANES, (j + 1) * LANES)
            rhs = jnp.concatenate([xs[:, sl], state_old[:, sl]], axis=0)
            ys = []
            for hh in range(2):
                ln = lane0 + 2 * j + hh
                col = jnp.broadcast_to(cum[:, ln:ln + 1], (q, q))
                seg = col - cum_t[ln:ln + 1, :]
                decay = jnp.exp(jnp.where(mask, seg, -jnp.inf))
                m = cb * decay * dt_t[ln:ln + 1, :]
                cs = c_g * jnp.exp(col)
                lhs = jnp.concatenate([m.astype(BF16), cs.astype(BF16)], axis=1)
                ys.append(_dot(lhs, rhs))
            lane = lax.broadcasted_iota(jnp.int32, (q, LANES), 1)
            y_ref[0, rs, sl] = jnp.where(lane < SSD_HEADDIM, ys[0], ys[1]).astype(BF16)

    dec = jnp.broadcast_to(jnp.exp(total), (8, LANES))
    dec_exp = _dot_split_lhs(dec, e_ref[...])[0:1, :]
    state_ref[...] = state_ref[...] * dec_exp + jnp.concatenate(new_states, axis=1)


def _dot_split_lhs_rhs(tri_bf16, da_f32):
    h, m, l = _split3(da_f32)
    return _dot(tri_bf16, h) + _dot(tri_bf16, m) + _dot(tri_bf16, l)


SSD_SUB = 4


def _ssd(conv_out, dt, dt_bias, a_log, expand, reverse):
    b, s, _ = conv_out.shape
    rows = SSD_SUB * CHUNK
    nc = s // rows
    cidx = (lambda c: nc - 1 - c) if reverse else (lambda c: c)
    bcol = D_INNER // (SSD_GROUPS * D_STATE)
    gn = SSD_GROUPS * D_STATE
    const = lambda w: pl.BlockSpec(w.shape, lambda bi, c: (0, 0))
    return pl.pallas_call(
        functools.partial(_ssd_body, reverse),
        grid=(b, nc),
        in_specs=[pl.BlockSpec((1, rows, D_INNER), lambda bi, c: (bi, cidx(c), 0)),
                  pl.BlockSpec((1, rows, gn), lambda bi, c: (bi, cidx(c), bcol)),
                  pl.BlockSpec((1, rows, gn), lambda bi, c: (bi, cidx(c), bcol + 1)),
                  pl.BlockSpec((1, rows, LANES), lambda bi, c: (bi, cidx(c), 0)),
                  const(dt_bias), const(a_log), const(expand)],
        out_specs=pl.BlockSpec((1, rows, D_INNER), lambda bi, c: (bi, cidx(c), 0)),
        out_shape=jax.ShapeDtypeStruct((b, s, D_INNER), BF16),
        scratch_shapes=[pltpu.VMEM((D_STATE, D_INNER), F32)],
        compiler_params=_cparams(("parallel", "arbitrary")),
        name="ssd_bwd" if reverse else "ssd_fwd",
    )(conv_out, conv_out, conv_out, dt, dt_bias, a_log, expand)


def _mix_body(x_ref, o_ref, yf_ref, yb_ref, xs_ref, z_ref, g_ref,
              woa_ref, wos_ref, wout_ref, dsk_ref, nw_ref, lg_ref, lb_ref, x1_ref, x1p_ref):
    attn = _dot(o_ref[...], woa_ref[...])
    y = yf_ref[...].astype(F32) + yb_ref[...].astype(F32) + dsk_ref[...] * xs_ref[...].astype(F32)
    y = y * z_ref[...].astype(F32)
    yn = y * lax.rsqrt(jnp.mean(y * y, axis=-1, keepdims=True) + RMS_EPS) * nw_ref[...]
    ssd = _dot(yn.astype(BF16), wos_ref[...])
    g = g_ref[...].astype(F32)
    mixed_in = g[:, :D_MODEL] * attn + g[:, D_MODEL:] * ssd
    mixed = _dot(mixed_in.astype(BF16), wout_ref[...])
    x1 = _layer_norm_rows(DN_ALPHA * x_ref[...] + mixed, lg_ref[...], lb_ref[...])
    x1_ref[...] = x1
    x1p_ref[...] = _pack_rows(x1)


def _mix(x2d, o, yf, yb, conv_out, z, g, woa, wos, wout, dsk, nw, lg, lb, tm=256):
    t = x2d.shape[0]
    const = lambda w: pl.BlockSpec(w.shape, lambda i: (0, 0))
    row = lambda n: pl.BlockSpec((tm, n), lambda i: (i, 0))
    return pl.pallas_call(
        _mix_body,
        grid=(t // tm,),
        in_specs=[row(D_MODEL), row(D_MODEL), row(D_INNER), row(D_INNER), row(D_INNER), row(D_INNER),
                  row(2 * D_MODEL), const(woa), const(wos), const(wout), const(dsk), const(nw),
                  const(lg), const(lb)],
        out_specs=[row(D_MODEL), row(PACKED_W)],
        out_shape=[jax.ShapeDtypeStruct((t, D_MODEL), F32), jax.ShapeDtypeStruct((t, PACKED_W), jnp.uint32)],
        compiler_params=_cparams(("parallel",)),
        name="mix_ln1",
    )(x2d, o, yf, yb, conv_out, z, g, woa, wos, wout, dsk, nw, lg, lb)


def _router_body(x_ref, wh_ref, wl_ref, bias_ref, su_ref, idx_ref, wts_ref, pos_ref, cnt_ref, carry_ref):
    @pl.when(pl.program_id(0) == 0)
    def _():
        carry_ref[...] = jnp.zeros_like(carry_ref)

    tm = x_ref.shape[0]
    x = x_ref[...]
    xh = x.astype(BF16)
    xl = (x - xh.astype(F32)).astype(BF16)
    wh = wh_ref[...]
    logits = _dot_nt(wh, xh) + _dot_nt(wh, xl) + _dot_nt(wl_ref[...], xh)
    scores = jax.nn.sigmoid(logits)
    choice = scores + bias_ref[...]
    gsz = N_EXPERTS // N_EXPERT_GROUPS
    shp = (N_EXPERT_GROUPS, gsz, tm)
    ch = choice.reshape(shp)
    sc = scores.reshape(shp)
    jio = lax.broadcasted_iota(jnp.int32, shp, 1)
    gio = lax.broadcasted_iota(jnp.int32, shp, 0)
    eio = gio * gsz + jio
    neg = -jnp.inf

    m1 = jnp.max(ch, axis=1, keepdims=True)
    i1 = jnp.min(jnp.where(ch == m1, jio, gsz), axis=1, keepdims=True)
    m2 = jnp.max(jnp.where(jio == i1, neg, ch), axis=1, keepdims=True)
    gs = m1 + m2
    gio1 = lax.broadcasted_iota(jnp.int32, (N_EXPERT_GROUPS, 1, tm), 0)
    sel = jnp.zeros(gs.shape, jnp.bool_)
    cur = gs
    for _ in range(TOPK_GROUPS):
        m = jnp.max(cur, axis=0, keepdims=True)
        gi = jnp.min(jnp.where(cur == m, gio1, N_EXPERT_GROUPS), axis=0, keepdims=True)
        hit = gio1 == gi
        sel = jnp.logical_or(sel, hit)
        cur = jnp.where(hit, neg, cur)
    masked = jnp.where(sel, ch, neg)

    def red(op, v):
        return op(op(v, axis=1, keepdims=True), axis=0, keepdims=True)

    hits, idxs, ws = [], [], []
    for _ in range(TOP_K):
        m = red(jnp.max, masked)
        ei = red(jnp.min, jnp.where(masked == m, eio, N_EXPERTS))
        hit = eio == ei
        hits.append(hit)
        idxs.append(ei)
        ws.append(red(jnp.sum, jnp.where(hit, sc, 0.0)))
        masked = jnp.where(hit, neg, masked)
    wsum = ws[0]
    for k in range(1, TOP_K):
        wsum = wsum + ws[k]

    onehot = hits[0]
    for k in range(1, TOP_K):
        onehot = jnp.logical_or(onehot, hits[k])
    oh = onehot.astype(F32).reshape(N_EXPERTS, tm)
    before = _dot(oh.astype(BF16), su_ref[...]) + carry_ref[...]
    before3 = before.reshape(shp)
    for k in range(TOP_K):
        idx_ref[k:k + 1, :] = idxs[k].reshape(1, tm)
        wts_ref[k:k + 1, :] = (ws[k] / wsum * ROUTED_SCALE).reshape(1, tm)
        pos_ref[k:k + 1, :] = red(jnp.sum, jnp.where(hits[k], before3, 0.0)).reshape(1, tm).astype(jnp.int32)
    carry_ref[...] = carry_ref[...] + jnp.sum(oh, axis=1, keepdims=True)
    cnt_ref[...] = jnp.broadcast_to(carry_ref[...], cnt_ref.shape)


def _router(x1, wr_hi, wr_lo, bias, su, tm=512):
    t = x1.shape[0]
    const = lambda w: pl.BlockSpec(w.shape, lambda i: (0, 0))
    out_tok = pl.BlockSpec((TOP_K, tm), lambda i: (0, i))
    return pl.pallas_call(
        _router_body,
        grid=(t // tm,),
        in_specs=[pl.BlockSpec((tm, D_MODEL), lambda i: (i, 0)), const(wr_hi), const(wr_lo), const(bias),
                  const(su)],
        out_specs=[out_tok, out_tok, out_tok, pl.BlockSpec((N_EXPERTS, LANES), lambda i: (0, 0))],
        out_shape=[jax.ShapeDtypeStruct((TOP_K, t), jnp.int32),
                   jax.ShapeDtypeStruct((TOP_K, t), F32),
                   jax.ShapeDtypeStruct((TOP_K, t), jnp.int32),
                   jax.ShapeDtypeStruct((N_EXPERTS, LANES), F32)],
        scratch_shapes=[pltpu.VMEM((N_EXPERTS, 1), F32)],
        compiler_params=_cparams(("arbitrary",)),
        name="router",
    )(x1, wr_hi, wr_lo, bias, su)


def _expert_body(be_ref, x_ref, wg_ref, wu_ref, wd_ref, y_ref):
    hi, lo = _unpack_rows(x_ref[...])
    xb = jnp.concatenate([hi.astype(BF16), lo.astype(BF16)], axis=1)
    h = _silu(_dot(xb, wg_ref[0])) * _dot(xb, wu_ref[0])
    y_ref[...] = _pack_rows(_dot(h.astype(BF16), wd_ref[0]))


def _experts(block_e, xs, wg, wu, wd):
    rows = xs.shape[0]
    nb = rows // EXPERT_BLOCK
    grid_spec = pltpu.PrefetchScalarGridSpec(
        num_scalar_prefetch=1,
        grid=(nb,),
        in_specs=[pl.BlockSpec((EXPERT_BLOCK, PACKED_W), lambda b, be: (b, 0)),
                  pl.BlockSpec((1, D_MODEL, D_EXPERT), lambda b, be: (be[b], 0, 0)),
                  pl.BlockSpec((1, D_MODEL, D_EXPERT), lambda b, be: (be[b], 0, 0)),
                  pl.BlockSpec((1, D_EXPERT, D_MODEL), lambda b, be: (be[b], 0, 0))],
        out_specs=pl.BlockSpec((EXPERT_BLOCK, PACKED_W), lambda b, be: (b, 0)),
    )
    return pl.pallas_call(
        _expert_body,
        grid_spec=grid_spec,
        out_shape=jax.ShapeDtypeStruct((rows, PACKED_W), jnp.uint32),
        compiler_params=_cparams(("parallel",)),
        name="moe_experts",
    )(block_e, xs, wg, wu, wd)


def _final_body(x1_ref, wt_ref, g_ref, wsg_ref, wsu_ref, wsd_ref, lg_ref, lb_ref, out_ref):
    x1 = x1_ref[...]
    xb = x1.astype(BF16)
    h = _silu(_dot(xb, wsg_ref[...])) * _dot(xb, wsu_ref[...])
    acc = DN_ALPHA * x1 + _dot(h.astype(BF16), wsd_ref[...])
    wt = wt_ref[...]
    acc_hi, acc_lo = acc[:, :PACKED_W], acc[:, PACKED_W:]
    for k in range(TOP_K):
        hi, lo = _unpack_rows(g_ref[k])
        acc_hi = acc_hi + wt[:, k:k + 1] * hi
        acc_lo = acc_lo + wt[:, k:k + 1] * lo
    acc = jnp.concatenate([acc_hi, acc_lo], axis=1)
    out_ref[...] = _layer_norm_rows(acc, lg_ref[...], lb_ref[...])


def _final(x1, wt_rows, g, wsg, wsu, wsd, lg, lb, tm=256):
    t = x1.shape[0]
    const = lambda w: pl.BlockSpec(w.shape, lambda i: (0, 0))
    return pl.pallas_call(
        _final_body,
        grid=(t // tm,),
        in_specs=[pl.BlockSpec((tm, D_MODEL), lambda i: (i, 0)),
                  pl.BlockSpec((tm, TOP_K), lambda i: (i, 0)),
                  pl.BlockSpec((TOP_K, tm, PACKED_W), lambda i: (0, i, 0)),
                  const(wsg), const(wsu), const(wsd), const(lg), const(lb)],
        out_specs=pl.BlockSpec((tm, D_MODEL), lambda i: (i, 0)),
        out_shape=jax.ShapeDtypeStruct((t, D_MODEL), F32),
        compiler_params=_cparams(("parallel",)),
        name="moe_combine_ln2",
    )(x1, wt_rows, g, wsg, wsu, wsd, lg, lb)


SC_CORES = 2
SC_SUBCORES = 16
SC_CHUNK = 128


def _sc_scatter_rows(src, dest, pad_rows, rows):
    t, w = src.shape
    workers = SC_CORES * SC_SUBCORES
    per_w = t // workers
    steps = per_w // SC_CHUNK
    pad_steps = pad_rows.shape[0] // (workers * SC_CHUNK)
    assert steps * SC_CHUNK * workers == t and pad_steps * SC_CHUNK * workers == pad_rows.shape[0]
    mesh = plsc.VectorSubcoreMesh(core_axis_name="c", subcore_axis_name="s")
    idx_tok = dest.reshape(TOP_K, workers, steps, SC_CHUNK).transpose(1, 2, 0, 3)
    idx_pad = pad_rows.reshape(workers, pad_steps, SC_CHUNK)
    zeros = jnp.zeros((SC_CHUNK, w), src.dtype)

    def body(src_hbm, tok_hbm, pad_hbm, zero_hbm, out_hbm, tok_v, pad_v, rows_v, sem):
        wid = lax.axis_index("s") * SC_CORES + lax.axis_index("c")
        pltpu.sync_copy(tok_hbm.at[wid], tok_v)
        pltpu.sync_copy(pad_hbm.at[wid], pad_v)

        @pl.loop(0, steps)
        def _(i):
            off = pl.multiple_of(wid * per_w + i * SC_CHUNK, SC_CHUNK)
            pltpu.sync_copy(src_hbm.at[pl.ds(off, SC_CHUNK)], rows_v)
            copies = [pltpu.async_copy(rows_v, out_hbm.at[tok_v.at[i, k]], sem) for k in range(TOP_K)]
            for c in copies:
                c.wait()

        pltpu.sync_copy(zero_hbm, rows_v)

        @pl.loop(0, pad_steps)
        def _(i):
            pltpu.async_copy(rows_v, out_hbm.at[pad_v.at[i]], sem).wait()

    return pl.kernel(
        body, mesh=mesh,
        out_type=jax.ShapeDtypeStruct((rows, w), src.dtype),
        scratch_types=[pltpu.VMEM((steps, TOP_K, SC_CHUNK), jnp.int32), pltpu.VMEM((pad_steps, SC_CHUNK), jnp.int32),
                       pltpu.VMEM((SC_CHUNK, w), src.dtype), pltpu.SemaphoreType.DMA],
    )(src, idx_tok, idx_pad, zeros)


def _sc_gather_rows(table, idx):
    n = idx.shape[0]
    w = table.shape[1]
    workers = SC_CORES * SC_SUBCORES
    per_w = n // workers
    steps = per_w // SC_CHUNK
    assert steps * SC_CHUNK * workers == n
    mesh = plsc.VectorSubcoreMesh(core_axis_name="c", subcore_axis_name="s")

    def body(table_hbm, idx_hbm, out_hbm, idx_v, rows_v, sem):
        wid = lax.axis_index("s") * SC_CORES + lax.axis_index("c")
        pltpu.sync_copy(idx_hbm.at[wid], idx_v)

        @pl.loop(0, steps)
        def _(i):
            off = pl.multiple_of(wid * per_w + i * SC_CHUNK, SC_CHUNK)
            pltpu.async_copy(table_hbm.at[idx_v.at[i]], rows_v, sem).wait()
            pltpu.sync_copy(rows_v, out_hbm.at[pl.ds(off, SC_CHUNK)])

    return pl.kernel(
        body, mesh=mesh,
        out_type=jax.ShapeDtypeStruct((n, w), table.dtype),
        scratch_types=[pltpu.VMEM((steps, SC_CHUNK), jnp.int32), pltpu.VMEM((SC_CHUNK, w), table.dtype),
                       pltpu.SemaphoreType.DMA],
    )(table, idx.reshape(workers, steps, SC_CHUNK))


def _dest_body(ps_ref, idx_ref, pos_ref, out_ref):
    idx = idx_ref[...]
    acc = pos_ref[...]
    for e in range(N_EXPERTS):
        acc = acc + jnp.where(idx == e, ps_ref[e], 0)
    out_ref[...] = acc


def _dest_rows(pad_start, idx, pos, tn=2048):
    t = idx.shape[1]
    tn = min(tn, t)
    tok = pl.BlockSpec((TOP_K, tn), lambda i, ps: (0, i))
    grid_spec = pltpu.PrefetchScalarGridSpec(
        num_scalar_prefetch=1, grid=(t // tn,), in_specs=[tok, tok], out_specs=tok)
    return pl.pallas_call(
        _dest_body,
        grid_spec=grid_spec,
        out_shape=jax.ShapeDtypeStruct((TOP_K, t), jnp.int32),
        compiler_params=_cparams(("parallel",)),
        name="moe_dest",
    )(pad_start, idx, pos)


def _rot_cols(w):
    half = w.shape[-1] // 2
    return jnp.concatenate([-w[..., half:], w[..., :half]], axis=-1)


def _prepare(p, seq_lens):
    f = {}
    w_in = p["w_in"]
    o = np.cumsum((0, Q_LORA, KV_LORA, QK_ROPE, D_INNER, CONV_DIM, SSD_HEADS, SSD_HEADS, D_MODEL, D_MODEL))
    seg = lambda i: w_in[:, o[i]:o[i + 1]]
    zeros = lambda n: jnp.zeros((D_MODEL, n), F32)
    f["wq"] = jnp.concatenate([seg(0), seg(1), seg(2), _rot_cols(seg(2)), zeros(QKV_W - KROPE_OFF - 2 * QK_ROPE)],
                              axis=1).astype(BF16)
    f["wz"] = seg(3).astype(BF16)
    f["wx"] = seg(4).astype(BF16)
    f["wdt"] = jnp.concatenate([seg(5), seg(6), zeros(LANES - 2 * SSD_HEADS)], axis=1).astype(BF16)
    f["wg"] = jnp.concatenate([seg(7), seg(8)], axis=1).astype(BF16)

    f["qn"] = p["q_norm_w"].reshape(1, Q_LORA)
    f["kvn"] = p["kv_norm_w"].reshape(1, KV_LORA)
    w_uq = p["w_uq"]
    nope, rope = w_uq[..., :QK_NOPE], w_uq[..., QK_NOPE:]
    padq = HEAD_PAD - QK_NOPE - QK_ROPE
    zq = lambda n: jnp.zeros((Q_LORA, N_HEADS, n), F32)
    wq_a = jnp.concatenate([nope, rope, zq(padq)], axis=-1).reshape(Q_LORA, N_HEADS * HEAD_PAD)
    wq_b = jnp.concatenate([zq(QK_NOPE), _rot_cols(rope), zq(padq)], axis=-1).reshape(Q_LORA, N_HEADS * HEAD_PAD)
    f["w_q"] = jnp.concatenate([wq_a, wq_b], axis=1).astype(BF16)
    w_ukv = p["w_ukv"]
    wk_t = jnp.transpose(w_ukv[..., :QK_NOPE], (1, 2, 0))
    eye = jnp.eye(QK_ROPE, dtype=F32)
    rope_rows = jnp.concatenate([jnp.zeros((QK_ROPE, KV_LORA), F32), eye, eye,
                                 jnp.zeros((QK_ROPE, LANES - 2 * QK_ROPE), F32)], axis=1)
    wk = jnp.concatenate([
        jnp.concatenate([wk_t, jnp.zeros((N_HEADS, QK_NOPE, LANES), F32)], axis=2),
        jnp.broadcast_to(rope_rows, (N_HEADS, QK_ROPE, KV_LORA + LANES)),
        jnp.zeros((N_HEADS, padq, KV_LORA + LANES), F32)], axis=1)
    f["w_k"] = wk.reshape(N_HEADS * HEAD_PAD, KV_LORA + LANES).astype(BF16)
    w_v = jnp.concatenate([w_ukv[..., QK_NOPE:], jnp.zeros((KV_LORA, N_HEADS, HEAD_PAD - V_DIM), F32)], axis=-1)
    f["w_v"] = w_v.reshape(KV_LORA, N_HEADS * HEAD_PAD).astype(BF16)
    f["v_one"] = (jnp.arange(N_HEADS * HEAD_PAD) % HEAD_PAD == V_DIM).astype(F32).reshape(1, -1)

    scale = (QK_NOPE + QK_ROPE) ** -0.5 * math.log2(math.e)
    f["rope"] = {}
    for s in sorted(set(seq_lens)):
        inv_freq = ROPE_THETA ** (-jnp.arange(0, QK_ROPE, 2, dtype=F32) / QK_ROPE)
        ang = jnp.arange(s, dtype=F32)[:, None] * inv_freq[None, :]
        cos, sin = jnp.cos(ang), jnp.sin(ang)
        one, zero = jnp.ones((s, QK_NOPE), F32), jnp.zeros((s, QK_NOPE), F32)
        zp = jnp.zeros((s, padq), F32)
        cq_t = scale * jnp.concatenate([one, cos, cos, zp], axis=1)
        sq_t = scale * jnp.concatenate([zero, sin, sin, zp], axis=1)
        tk_t = jnp.concatenate([cos, cos, sin, sin, jnp.zeros((s, LANES - 2 * QK_ROPE), F32)], axis=1)
        f["rope"][s] = (cq_t, sq_t, tk_t)

    f["conv_w"] = jnp.concatenate([p["conv_w"], jnp.zeros((8 - CONV_WIDTH, CONV_DIM), F32)], axis=0)
    f["conv_b"] = p["conv_b"].reshape(1, CONV_DIM)
    pad_h = jnp.zeros((LANES - 2 * SSD_HEADS,), F32)
    f["dt_bias"] = jnp.concatenate([p["dt_bias_f"], p["dt_bias_b"], pad_h]).reshape(1, LANES)
    f["a_log"] = jnp.concatenate([p["a_log_f"], p["a_log_b"], pad_h]).reshape(1, LANES)
    head_of_lane = jnp.arange(D_INNER, dtype=jnp.int32) // SSD_HEADDIM
    lane_id = jnp.arange(LANES, dtype=jnp.int32)[:, None]
    f["expand_f"] = (lane_id == head_of_lane[None, :]).astype(BF16)
    f["expand_b"] = (lane_id == head_of_lane[None, :] + SSD_HEADS).astype(BF16)
    f["d_skip"] = jnp.repeat(p["d_skip"], SSD_HEADDIM).reshape(1, D_INNER)
    f["ssd_nw"] = p["ssd_norm_w"].reshape(1, D_INNER)
    f["w_oa"] = p["w_o_attn"].astype(BF16)
    f["w_os"] = p["w_o_ssd"].astype(BF16)
    f["w_out"] = p["w_out"].astype(BF16)
    f["ln1_g"] = p["ln1_g"].reshape(1, D_MODEL)
    f["ln1_b"] = p["ln1_b"].reshape(1, D_MODEL)

    wr_t = p["w_router"].T
    wr_hi = wr_t.astype(BF16)
    f["wr_hi"] = wr_hi
    f["wr_lo"] = (wr_t - wr_hi.astype(F32)).astype(BF16)
    f["r_bias"] = p["router_bias"].reshape(N_EXPERTS, 1)
    f["w_gate"] = p["w_gate"].astype(BF16)
    f["w_up"] = p["w_up"].astype(BF16)
    f["w_down"] = p["w_down"].astype(BF16)
    f["ws_gate"] = p["ws_gate"].astype(BF16)
    f["ws_up"] = p["ws_up"].astype(BF16)
    f["ws_down"] = p["ws_down"].astype(BF16)
    f["ln2_g"] = p["ln2_g"].reshape(1, D_MODEL)
    f["ln2_b"] = p["ln2_b"].reshape(1, D_MODEL)
    return f


ROUTER_TILE = 512


def _layer_front(x, f):
    b, s, d = x.shape
    t = b * s
    x2d = x.reshape(t, d)
    qkv, z, xbc, dt, g = _in_proj(x2d, f["wq"], f["wz"], f["wx"], f["wdt"], f["wg"])

    cq_t, sq_t, tk_t = f["rope"][s]
    q, kt, v = _mla_prep(qkv.reshape(b, s, QKV_W), f["qn"], f["kvn"], f["w_q"], f["w_k"], f["w_v"], f["v_one"],
                         cq_t, sq_t, tk_t)
    o = _attention(q, kt, v)

    conv_out = _conv(xbc.reshape(b, s, CONV_DIM), f["conv_w"], f["conv_b"])
    dt3 = dt.reshape(b, s, LANES)
    y_f = _ssd(conv_out, dt3, f["dt_bias"], f["a_log"], f["expand_f"], reverse=False)
    y_b = _ssd(conv_out, dt3, f["dt_bias"], f["a_log"], f["expand_b"], reverse=True)

    x1, x1p = _mix(x2d, o.reshape(t, -1), y_f.reshape(t, -1), y_b.reshape(t, -1), conv_out.reshape(t, -1), z, g,
              f["w_oa"], f["w_os"], f["w_out"], f["d_skip"], f["ssd_nw"], f["ln1_g"], f["ln1_b"])

    su = (jnp.arange(ROUTER_TILE)[:, None] < jnp.arange(ROUTER_TILE)[None, :]).astype(BF16)
    idx, wts, pos, cnt = _router(x1, f["wr_hi"], f["wr_lo"], f["r_bias"], su, tm=ROUTER_TILE)

    counts = cnt[:, 0].astype(jnp.int32)
    padded = (counts + EXPERT_BLOCK - 1) // EXPERT_BLOCK * EXPERT_BLOCK
    pad_end = jnp.cumsum(padded)
    pad_start = pad_end - padded
    nb = t * TOP_K // EXPERT_BLOCK + N_EXPERTS
    n_used = pad_end[-1] // EXPERT_BLOCK
    blk_row = jnp.minimum(jnp.arange(nb, dtype=jnp.int32), n_used - 1) * EXPERT_BLOCK
    block_e = jnp.minimum(jnp.sum((blk_row[:, None] >= pad_end[None, :]).astype(jnp.int32), axis=1), N_EXPERTS - 1)

    seg_len = jnp.concatenate([padded - counts, (nb * EXPERT_BLOCK - pad_end[-1]).reshape(1)])
    seg_first = jnp.concatenate([pad_start + counts, pad_end[-1:]])
    seg_end = jnp.cumsum(seg_len)
    j = jnp.arange(N_EXPERTS * EXPERT_BLOCK, dtype=jnp.int32)
    onehot = (jnp.sum((j[:, None] >= seg_end[None, :]).astype(jnp.int32), axis=1)[:, None]
              == jnp.arange(N_EXPERTS + 1, dtype=jnp.int32)[None, :])
    pad_rows = j + jnp.sum(jnp.where(onehot, (seg_first - (seg_end - seg_len))[None, :], 0), axis=1)

    dest = _dest_rows(pad_start, idx, pos)
    xs = _sc_scatter_rows(x1p, dest, pad_rows.astype(jnp.int32), nb * EXPERT_BLOCK)
    return dict(shape=(b, s, d), x1=x1, wt_rows=wts.T, dest=dest, block_e=block_e, xs=xs)


def _layer_experts(st, f):
    t = st["x1"].shape[0]
    ys = _experts(st["block_e"], st["xs"], f["w_gate"], f["w_up"], f["w_down"])
    return _sc_gather_rows(ys, st["dest"].reshape(TOP_K * t)).reshape(TOP_K, t, PACKED_W)


def _layer_combine(st, g, f):
    y = _final(st["x1"], st["wt_rows"], g, f["ws_gate"], f["ws_up"], f["ws_down"], f["ln2_g"], f["ln2_b"])
    return y.reshape(st["shape"])


def kernel(x_prompt, x_sample, w_in, q_norm_w, kv_norm_w, w_uq, w_ukv, w_o_attn, conv_w, conv_b, dt_bias_f,
           dt_bias_b, a_log_f, a_log_b, d_skip, ssd_norm_w, w_o_ssd, w_out, ln1_g, ln1_b, w_router, router_bias,
           w_gate, w_up, w_down, ws_gate, ws_up, ws_down, ln2_g, ln2_b):
    params = dict(w_in=w_in, q_norm_w=q_norm_w, kv_norm_w=kv_norm_w, w_uq=w_uq, w_ukv=w_ukv, w_o_attn=w_o_attn,
                  conv_w=conv_w, conv_b=conv_b, dt_bias_f=dt_bias_f, dt_bias_b=dt_bias_b, a_log_f=a_log_f,
                  a_log_b=a_log_b, d_skip=d_skip, ssd_norm_w=ssd_norm_w, w_o_ssd=w_o_ssd, w_out=w_out,
                  ln1_g=ln1_g, ln1_b=ln1_b, w_router=w_router, router_bias=router_bias, w_gate=w_gate,
                  w_up=w_up, w_down=w_down, ws_gate=ws_gate, ws_up=ws_up, ws_down=ws_down, ln2_g=ln2_g,
                  ln2_b=ln2_b)
    assert w_in.shape[0] == DEPTH == 1
    params = {k: v[0] for k, v in params.items()}
    f = _prepare(params, (x_prompt.shape[1], x_sample.shape[1]))
    st_s = _layer_front(x_sample, f)
    st_p = _layer_front(x_prompt, f)
    g_s = _layer_experts(st_s, f)
    g_p = _layer_experts(st_p, f)
    y_s = _layer_combine(st_s, g_s, f)
    y_p = _layer_combine(st_p, g_p, f)
    return (y_p, y_s)
```

```python
import functools
import math

import jax
import jax.numpy as jnp
import numpy as np
from jax import lax
from jax.experimental import pallas as pl
from jax.experimental.pallas import tpu as pltpu
from jax.experimental.pallas import tpu_sc as plsc

F32 = jnp.float32
BF16 = jnp.bfloat16

D_MODEL = 1024
DEPTH = 1
N_HEADS = 16
QK_NOPE = 64
QK_ROPE = 32
V_DIM = 64
Q_LORA = 384
KV_LORA = 256
ROPE_THETA = 10000.0
D_INNER = 2 * D_MODEL
SSD_HEADDIM = 64
SSD_HEADS = D_INNER // SSD_HEADDIM
SSD_GROUPS = 4
D_STATE = 128
CONV_WIDTH = 5
CONV_DIM = D_INNER + 2 * SSD_GROUPS * D_STATE
CHUNK = 128
N_EXPERTS = 64
TOP_K = 8
N_EXPERT_GROUPS = 8
TOPK_GROUPS = 4
D_EXPERT = D_MODEL // 4
D_SHARED = D_MODEL // 4
ROUTED_SCALE = 2.5
LN_EPS = 1e-5
RMS_EPS = 1e-6
DN_ALPHA = (2 * DEPTH) ** 0.25

LANES = 128
HEAD_PAD = 128
QKV_W = 768
KROPE_OFF = Q_LORA + KV_LORA
EXPERT_BLOCK = 1024
PACKED_W = D_MODEL // 2
VMEM_LIMIT = 56 * 1024 * 1024


def _cparams(sem, vmem=VMEM_LIMIT):
    return pltpu.CompilerParams(dimension_semantics=sem, vmem_limit_bytes=vmem)


def _dot(a, b):
    return jnp.dot(a, b, preferred_element_type=F32)


def _dot_nt(a, b):
    return lax.dot_general(a, b, (((1,), (1,)), ((), ())), preferred_element_type=F32)


def _split3(a):
    h = a.astype(BF16)
    r = a - h.astype(F32)
    m = r.astype(BF16)
    l = (r - m.astype(F32)).astype(BF16)
    return h, m, l


def _dot_split_lhs(a_f32, b_bf16):
    h, m, l = _split3(a_f32)
    return _dot(h, b_bf16) + _dot(m, b_bf16) + _dot(l, b_bf16)


def _silu(x):
    return x * jax.nn.sigmoid(x)


def _pack_rows(x):
    n = x.shape[1] // 2
    hi = lax.bitcast_convert_type(x[:, :n].astype(BF16).astype(F32), jnp.uint32)
    lo = lax.bitcast_convert_type(x[:, n:].astype(BF16).astype(F32), jnp.uint32)
    return hi | (lo >> 16)


def _unpack_rows(w):
    hi = lax.bitcast_convert_type(w & jnp.uint32(0xFFFF0000), F32)
    lo = lax.bitcast_convert_type(w << 16, F32)
    return hi, lo


def _layer_norm_rows(r, g, b):
    mu = jnp.mean(r, axis=-1, keepdims=True)
    d = r - mu
    var = jnp.mean(d * d, axis=-1, keepdims=True)
    return d * lax.rsqrt(var + LN_EPS) * g + b


def _inproj_body(x_ref, wq_ref, wz_ref, wx_ref, wdt_ref, wg_ref,
                 qkv_ref, z_ref, xbc_ref, dt_ref, g_ref):
    xb = x_ref[...].astype(BF16)
    qkv_ref[...] = _dot(xb, wq_ref[...]).astype(BF16)
    z_ref[...] = _silu(_dot(xb, wz_ref[...])).astype(BF16)
    xbc_ref[...] = _dot(xb, wx_ref[...]).astype(BF16)
    dt_ref[...] = _dot(xb, wdt_ref[...])
    g_ref[...] = jax.nn.sigmoid(_dot(xb, wg_ref[...])).astype(BF16)


def _in_proj(x2d, wq, wz, wx, wdt, wg, tm=512):
    t = x2d.shape[0]
    const = lambda w: pl.BlockSpec(w.shape, lambda i: (0, 0), pipeline_mode=pl.Buffered(1))
    row = lambda n: pl.BlockSpec((tm, n), lambda i: (i, 0))
    return pl.pallas_call(
        _inproj_body,
        grid=(t // tm,),
        in_specs=[row(D_MODEL), const(wq), const(wz), const(wx), const(wdt), const(wg)],
        out_specs=[row(QKV_W), row(D_INNER), row(CONV_DIM), row(LANES), row(2 * D_MODEL)],
        out_shape=[jax.ShapeDtypeStruct((t, QKV_W), BF16),
                   jax.ShapeDtypeStruct((t, D_INNER), BF16),
                   jax.ShapeDtypeStruct((t, CONV_DIM), BF16),
                   jax.ShapeDtypeStruct((t, LANES), F32),
                   jax.ShapeDtypeStruct((t, 2 * D_MODEL), BF16)],
        compiler_params=_cparams(("parallel",)),
        name="in_proj",
    )(x2d, wq, wz, wx, wdt, wg)


def _mla_prep_body(qkv_ref, qn_ref, kvn_ref, wq_ref, wk_ref, wv_ref, vone_ref, cq_ref, sq_ref, tk_ref,
                   q_ref, kt_ref, v_ref):
    qkv = qkv_ref[0].astype(F32)
    cq = qkv[:, :Q_LORA]
    cq = cq * lax.rsqrt(jnp.mean(cq * cq, axis=-1, keepdims=True) + RMS_EPS) * qn_ref[...]
    ckv = qkv[:, Q_LORA:KROPE_OFF]
    ckv = ckv * lax.rsqrt(jnp.mean(ckv * ckv, axis=-1, keepdims=True) + RMS_EPS) * kvn_ref[...]
    ckv_b = ckv.astype(BF16)
    qq = _dot(cq.astype(BF16), wq_ref[...])
    half = N_HEADS * HEAD_PAD
    cq_t = cq_ref[...]
    sq_t = sq_ref[...]
    for h in range(N_HEADS):
        lo = h * HEAD_PAD
        q_ref[0, :, lo:lo + HEAD_PAD] = (qq[:, lo:lo + HEAD_PAD] * cq_t
                                          + qq[:, half + lo:half + lo + HEAD_PAD] * sq_t).astype(BF16)
    kr = (qkv[:, KROPE_OFF:] * tk_ref[...]).astype(BF16)
    e = jnp.concatenate([ckv_b, kr], axis=1)
    kt_ref[0] = _dot_nt(wk_ref[...], e).astype(BF16)
    v_ref[0] = (_dot(ckv_b, wv_ref[...]) + vone_ref[...]).astype(BF16)


def _mla_prep(qkv, qn, kvn, wq, wk, wv, vone, cq_t, sq_t, tk_t, ts=512):
    b, s, _ = qkv.shape
    const = lambda w: pl.BlockSpec(w.shape, lambda bi, i: (0, 0))
    tab = pl.BlockSpec((ts, LANES), lambda bi, i: (i, 0))
    hw = N_HEADS * HEAD_PAD
    return pl.pallas_call(
        _mla_prep_body,
        grid=(b, s // ts),
        in_specs=[pl.BlockSpec((1, ts, QKV_W), lambda bi, i: (bi, i, 0)),
                  const(qn), const(kvn), const(wq), const(wk), const(wv), const(vone), tab, tab, tab],
        out_specs=[pl.BlockSpec((1, ts, hw), lambda bi, i: (bi, i, 0)),
                   pl.BlockSpec((1, hw, ts), lambda bi, i: (bi, 0, i)),
                   pl.BlockSpec((1, ts, hw), lambda bi, i: (bi, i, 0))],
        out_shape=[jax.ShapeDtypeStruct((b, s, hw), BF16),
                   jax.ShapeDtypeStruct((b, hw, s), BF16),
                   jax.ShapeDtypeStruct((b, s, hw), BF16)],
        compiler_params=_cparams(("parallel", "parallel")),
        name="mla_prep",
    )(qkv, qn, kvn, wq, wk, wv, vone, cq_t, sq_t, tk_t)


def _attn_body(q_ref, kt_ref, v_ref, o_ref):
    tq = q_ref.shape[1]
    for r in range(tq // ATTN_ROWS):
        rows = slice(r * ATTN_ROWS, (r + 1) * ATTN_ROWS)
        outs = []
        for hh in range(2):
            sl = slice(hh * HEAD_PAD, (hh + 1) * HEAD_PAD)
            s = _dot(q_ref[0, rows, sl], kt_ref[0, sl, :])
            m = jnp.max(s, axis=-1, keepdims=True)
            p = jnp.exp2(s - m).astype(BF16)
            o = _dot(p, v_ref[0, :, sl])
            outs.append(o / o[:, V_DIM:V_DIM + 1])
        lane = lax.broadcasted_iota(jnp.int32, outs[0].shape, 1)
        o_ref[0, rows, :] = jnp.where(lane < V_DIM, outs[0], pltpu.roll(outs[1], V_DIM, axis=1)).astype(BF16)


ATTN_ROWS = 256


def _attention(q, kt, v, tq=1024):
    b, s, _ = q.shape
    pairs = N_HEADS // 2
    return pl.pallas_call(
        _attn_body,
        grid=(b, pairs, s // tq),
        in_specs=[pl.BlockSpec((1, tq, 2 * HEAD_PAD), lambda bi, j, i: (bi, i, j)),
                  pl.BlockSpec((1, 2 * HEAD_PAD, s), lambda bi, j, i: (bi, j, 0)),
                  pl.BlockSpec((1, s, 2 * HEAD_PAD), lambda bi, j, i: (bi, 0, j))],
        out_specs=pl.BlockSpec((1, tq, 2 * V_DIM), lambda bi, j, i: (bi, i, j)),
        out_shape=jax.ShapeDtypeStruct((b, s, N_HEADS * V_DIM), BF16),
        compiler_params=_cparams(("parallel", "parallel", "arbitrary")),
        name="attention",
    )(q, kt, v)


CONV_HALO = 16


def _conv_body(prev_ref, main_ref, next_ref, w_ref, b_ref, o_ref, ext_ref):
    i = pl.program_id(1)
    n = pl.num_programs(1)
    ts = main_ref.shape[1]
    prev = prev_ref[0].astype(F32)
    nxt = next_ref[0].astype(F32)
    ext_ref[0:CONV_HALO, :] = jnp.where(i > 0, prev, 0.0)
    ext_ref[CONV_HALO:CONV_HALO + ts, :] = main_ref[0].astype(F32)
    ext_ref[CONV_HALO + ts:, :] = jnp.where(i < n - 1, nxt, 0.0)
    acc = b_ref[...]
    for k in range(CONV_WIDTH):
        off = CONV_HALO - CONV_WIDTH // 2 + k
        acc = acc + ext_ref[off:off + ts, :] * w_ref[k:k + 1, :]
    o_ref[0] = _silu(acc).astype(BF16)


def _conv(xbc, w8, bias, ts=1024, tc=1024):
    b, s, c = xbc.shape
    r = ts // CONV_HALO
    last = s // CONV_HALO - 1
    return pl.pallas_call(
        _conv_body,
        grid=(b, s // ts, c // tc),
        in_specs=[pl.BlockSpec((1, CONV_HALO, tc), lambda bi, i, j: (bi, jnp.maximum(i * r - 1, 0), j)),
                  pl.BlockSpec((1, ts, tc), lambda bi, i, j: (bi, i, j)),
                  pl.BlockSpec((1, CONV_HALO, tc), lambda bi, i, j: (bi, jnp.minimum((i + 1) * r, last), j)),
                  pl.BlockSpec((8, tc), lambda bi, i, j: (0, j)),
                  pl.BlockSpec((1, tc), lambda bi, i, j: (0, j))],
        out_specs=pl.BlockSpec((1, ts, tc), lambda bi, i, j: (bi, i, j)),
        out_shape=jax.ShapeDtypeStruct((b, s, c), BF16),
        scratch_shapes=[pltpu.VMEM((ts + 2 * CONV_HALO, tc), F32)],
        compiler_params=_cparams(("parallel", "parallel", "parallel")),
        name="conv",
    )(xbc, xbc, xbc, w8, bias)


def _ssd_body(reverse, xs_ref, b_ref, c_ref, dt_ref, dtb_ref, alog_ref, e_ref, y_ref, state_ref):
    @pl.when(pl.program_id(1) == 0)
    def _():
        state_ref[...] = jnp.zeros_like(state_ref)

    n_sub = xs_ref.shape[1] // CHUNK
    for cc in (range(n_sub - 1, -1, -1) if reverse else range(n_sub)):
        _ssd_chunk(reverse, slice(cc * CHUNK, (cc + 1) * CHUNK),
                   xs_ref, b_ref, c_ref, dt_ref, dtb_ref, alog_ref, e_ref, y_ref, state_ref)


def _ssd_chunk(reverse, rs, xs_ref, b_ref, c_ref, dt_ref, dtb_ref, alog_ref, e_ref, y_ref, state_ref):
    q = CHUNK
    lane0 = SSD_HEADS if reverse else 0
    rows = lax.broadcasted_iota(jnp.int32, (q, q), 0)
    cols = lax.broadcasted_iota(jnp.int32, (q, q), 1)
    mask = (rows <= cols) if reverse else (rows >= cols)
    tri = mask.astype(BF16)

    x = dt_ref[0, rs, :] + dtb_ref[...]
    dt = jnp.maximum(x, 0.0) + jnp.log(1.0 + jnp.exp(-jnp.abs(x)))
    da = dt * (-jnp.exp(alog_ref[...]))
    cum = _dot_split_lhs_rhs(tri, da)
    cum_t = cum.T
    dt_t = dt.T
    total = cum[0:1, :] if reverse else cum[q - 1:q, :]
    w = dt * jnp.exp(total - cum)
    w_exp = _dot(w.astype(BF16), e_ref[...])
    xs = xs_ref[0, rs, :]
    x_dec = (xs.astype(F32) * w_exp).astype(BF16)

    bm = b_ref[0, rs, :].astype(F32)
    cm = c_ref[0, rs, :]
    cm_f = cm.astype(F32)
    state_old = state_ref[...].astype(BF16)
    gw = D_STATE
    hp = SSD_HEADS // SSD_GROUPS * SSD_HEADDIM
    new_states = []
    for g in range(SSD_GROUPS):
        bt = bm[:, g * gw:(g + 1) * gw].T.astype(BF16)
        new_states.append(_dot(bt, x_dec[:, g * hp:(g + 1) * hp]))
        cb = _dot(cm[:, g * gw:(g + 1) * gw], bt)
        c_g = cm_f[:, g * gw:(g + 1) * gw]
        for jj in range(hp // LANES):
            j = g * (hp // LANES) + jj
            sl = slice(j * LANES, (j + 1) * LANES)
            rhs = jnp.concatenate([xs[:, sl], state_old[:, sl]], axis=0)
            ys = []
            for hh in range(2):
                ln = lane0 + 2 * j + hh
                col = jnp.broadcast_to(cum[:, ln:ln + 1], (q, q))
                seg = col - cum_t[ln:ln + 1, :]
                decay = jnp.exp(jnp.where(mask, seg, -jnp.inf))
                m = cb * decay * dt_t[ln:ln + 1, :]
                cs = c_g * jnp.exp(col)
                lhs = jnp.concatenate([m.astype(BF16), cs.astype(BF16)], axis=1)
                ys.append(_dot(lhs, rhs))
            lane = lax.broadcasted_iota(jnp.int32, (q, LANES), 1)
            y_ref[0, rs, sl] = jnp.where(lane < SSD_HEADDIM, ys[0], ys[1]).astype(BF16)

    dec = jnp.broadcast_to(jnp.exp(total), (8, LANES))
    dec_exp = _dot_split_lhs(dec, e_ref[...])[0:1, :]
    state_ref[...] = state_ref[...] * dec_exp + jnp.concatenate(new_states, axis=1)


def _dot_split_lhs_rhs(tri_bf16, da_f32):
    h, m, l = _split3(da_f32)
    return _dot(tri_bf16, h) + _dot(tri_bf16, m) + _dot(tri_bf16, l)


SSD_SUB = 4


def _ssd(conv_out, dt, dt_bias, a_log, expand, reverse):
    b, s, _ = conv_out.shape
    rows = SSD_SUB * CHUNK
    nc = s // rows
    cidx = (lambda c: nc - 1 - c) if reverse else (lambda c: c)
    bcol = D_INNER // (SSD_GROUPS * D_STATE)
    gn = SSD_GROUPS * D_STATE
    const = lambda w: pl.BlockSpec(w.shape, lambda bi, c: (0, 0))
    return pl.pallas_call(
        functools.partial(_ssd_body, reverse),
        grid=(b, nc),
        in_specs=[pl.BlockSpec((1, rows, D_INNER), lambda bi, c: (bi, cidx(c), 0)),
                  pl.BlockSpec((1, rows, gn), lambda bi, c: (bi, cidx(c), bcol)),
                  pl.BlockSpec((1, rows, gn), lambda bi, c: (bi, cidx(c), bcol + 1)),
                  pl.BlockSpec((1, rows, LANES), lambda bi, c: (bi, cidx(c), 0)),
                  const(dt_bias), const(a_log), const(expand)],
        out_specs=pl.BlockSpec((1, rows, D_INNER), lambda bi, c: (bi, cidx(c), 0)),
        out_shape=jax.ShapeDtypeStruct((b, s, D_INNER), BF16),
        scratch_shapes=[pltpu.VMEM((D_STATE, D_INNER), F32)],
        compiler_params=_cparams(("parallel", "arbitrary")),
        name="ssd_bwd" if reverse else "ssd_fwd",
    )(conv_out, conv_out, conv_out, dt, dt_bias, a_log, expand)


def _mix_body(x_ref, o_ref, yf_ref, yb_ref, xs_ref, z_ref, g_ref,
              woa_ref, wos_ref, wout_ref, dsk_ref, nw_ref, lg_ref, lb_ref, x1_ref, x1p_ref):
    attn = _dot(o_ref[...], woa_ref[...])
    y = yf_ref[...].astype(F32) + yb_ref[...].astype(F32) + dsk_ref[...] * xs_ref[...].astype(F32)
    y = y * z_ref[...].astype(F32)
    yn = y * lax.rsqrt(jnp.mean(y * y, axis=-1, keepdims=True) + RMS_EPS) * nw_ref[...]
    ssd = _dot(yn.astype(BF16), wos_ref[...])
    g = g_ref[...].astype(F32)
    mixed_in = g[:, :D_MODEL] * attn + g[:, D_MODEL:] * ssd
    mixed = _dot(mixed_in.astype(BF16), wout_ref[...])
    x1 = _layer_norm_rows(DN_ALPHA * x_ref[...] + mixed, lg_ref[...], lb_ref[...])
    x1_ref[...] = x1
    x1p_ref[...] = _pack_rows(x1)


def _mix(x2d, o, yf, yb, conv_out, z, g, woa, wos, wout, dsk, nw, lg, lb, tm=256):
    t = x2d.shape[0]
    const = lambda w: pl.BlockSpec(w.shape, lambda i: (0, 0))
    row = lambda n: pl.BlockSpec((tm, n), lambda i: (i, 0))
    return pl.pallas_call(
        _mix_body,
        grid=(t // tm,),
        in_specs=[row(D_MODEL), row(D_MODEL), row(D_INNER), row(D_INNER), row(D_INNER), row(D_INNER),
                  row(2 * D_MODEL), const(woa), const(wos), const(wout), const(dsk), const(nw),
                  const(lg), const(lb)],
        out_specs=[row(D_MODEL), row(PACKED_W)],
        out_shape=[jax.ShapeDtypeStruct((t, D_MODEL), F32), jax.ShapeDtypeStruct((t, PACKED_W), jnp.uint32)],
        compiler_params=_cparams(("parallel",)),
        name="mix_ln1",
    )(x2d, o, yf, yb, conv_out, z, g, woa, wos, wout, dsk, nw, lg, lb)


def _router_body(x_ref, wh_ref, wl_ref, bias_ref, su_ref, idx_ref, wts_ref, pos_ref, cnt_ref, carry_ref):
    @pl.when(pl.program_id(0) == 0)
    def _():
        carry_ref[...] = jnp.zeros_like(carry_ref)

    tm = x_ref.shape[0]
    x = x_ref[...]
    xh = x.astype(BF16)
    xl = (x - xh.astype(F32)).astype(BF16)
    wh = wh_ref[...]
    logits = _dot_nt(wh, xh) + _dot_nt(wh, xl) + _dot_nt(wl_ref[...], xh)
    scores = jax.nn.sigmoid(logits)
    choice = scores + bias_ref[...]
    gsz = N_EXPERTS // N_EXPERT_GROUPS
    shp = (N_EXPERT_GROUPS, gsz, tm)
    ch = choice.reshape(shp)
    sc = scores.reshape(shp)
    jio = lax.broadcasted_iota(jnp.int32, shp, 1)
    gio = lax.broadcasted_iota(jnp.int32, shp, 0)
    eio = gio * gsz + jio
    neg = -jnp.inf

    m1 = jnp.max(ch, axis=1, keepdims=True)
    i1 = jnp.min(jnp.where(ch == m1, jio, gsz), axis=1, keepdims=True)
    m2 = jnp.max(jnp.where(jio == i1, neg, ch), axis=1, keepdims=True)
    gs = m1 + m2
    gio1 = lax.broadcasted_iota(jnp.int32, (N_EXPERT_GROUPS, 1, tm), 0)
    sel = jnp.zeros(gs.shape, jnp.bool_)
    cur = gs
    for _ in range(TOPK_GROUPS):
        m = jnp.max(cur, axis=0, keepdims=True)
        gi = jnp.min(jnp.where(cur == m, gio1, N_EXPERT_GROUPS), axis=0, keepdims=True)
        hit = gio1 == gi
        sel = jnp.logical_or(sel, hit)
        cur = jnp.where(hit, neg, cur)
    masked = jnp.where(sel, ch, neg)

    def red(op, v):
        return op(op(v, axis=1, keepdims=True), axis=0, keepdims=True)

    hits, idxs, ws = [], [], []
    for _ in range(TOP_K):
        m = red(jnp.max, masked)
        ei = red(jnp.min, jnp.where(masked == m, eio, N_EXPERTS))
        hit = eio == ei
        hits.append(hit)
        idxs.append(ei)
        ws.append(red(jnp.sum, jnp.where(hit, sc, 0.0)))
        masked = jnp.where(hit, neg, masked)
    wsum = ws[0]
    for k in range(1, TOP_K):
        wsum = wsum + ws[k]

    onehot = hits[0]
    for k in range(1, TOP_K):
        onehot = jnp.logical_or(onehot, hits[k])
    oh = onehot.astype(F32).reshape(N_EXPERTS, tm)
    before = _dot(oh.astype(BF16), su_ref[...]) + carry_ref[...]
    before3 = before.reshape(shp)
    for k in range(TOP_K):
        idx_ref[k:k + 1, :] = idxs[k].reshape(1, tm)
        wts_ref[k:k + 1, :] = (ws[k] / wsum * ROUTED_SCALE).reshape(1, tm)
        pos_ref[k:k + 1, :] = red(jnp.sum, jnp.where(hits[k], before3, 0.0)).reshape(1, tm).astype(jnp.int32)
    carry_ref[...] = carry_ref[...] + jnp.sum(oh, axis=1, keepdims=True)
    cnt_ref[...] = jnp.broadcast_to(carry_ref[...], cnt_ref.shape)


def _router(x1, wr_hi, wr_lo, bias, su, tm=512):
    t = x1.shape[0]
    const = lambda w: pl.BlockSpec(w.shape, lambda i: (0, 0))
    out_tok = pl.BlockSpec((TOP_K, tm), lambda i: (0, i))
    return pl.pallas_call(
        _router_body,
        grid=(t // tm,),
        in_specs=[pl.BlockSpec((tm, D_MODEL), lambda i: (i, 0)), const(wr_hi), const(wr_lo), const(bias),
                  const(su)],
        out_specs=[out_tok, out_tok, out_tok, pl.BlockSpec((N_EXPERTS, LANES), lambda i: (0, 0))],
        out_shape=[jax.ShapeDtypeStruct((TOP_K, t), jnp.int32),
                   jax.ShapeDtypeStruct((TOP_K, t), F32),
                   jax.ShapeDtypeStruct((TOP_K, t), jnp.int32),
                   jax.ShapeDtypeStruct((N_EXPERTS, LANES), F32)],
        scratch_shapes=[pltpu.VMEM((N_EXPERTS, 1), F32)],
        compiler_params=_cparams(("arbitrary",)),
        name="router",
    )(x1, wr_hi, wr_lo, bias, su)


def _expert_body(be_ref, x_ref, wg_ref, wu_ref, wd_ref, y_ref):
    hi, lo = _unpack_rows(x_ref[...])
    xb = jnp.concatenate([hi.astype(BF16), lo.astype(BF16)], axis=1)
    h = _silu(_dot(xb, wg_ref[0])) * _dot(xb, wu_ref[0])
    y_ref[...] = _pack_rows(_dot(h.astype(BF16), wd_ref[0]))


def _experts(block_e, xs, wg, wu, wd):
    rows = xs.shape[0]
    nb = rows // EXPERT_BLOCK
    grid_spec = pltpu.PrefetchScalarGridSpec(
        num_scalar_prefetch=1,
        grid=(nb,),
        in_specs=[pl.BlockSpec((EXPERT_BLOCK, PACKED_W), lambda b, be: (b, 0)),
                  pl.BlockSpec((1, D_MODEL, D_EXPERT), lambda b, be: (be[b], 0, 0)),
                  pl.BlockSpec((1, D_MODEL, D_EXPERT), lambda b, be: (be[b], 0, 0)),
                  pl.BlockSpec((1, D_EXPERT, D_MODEL), lambda b, be: (be[b], 0, 0))],
        out_specs=pl.BlockSpec((EXPERT_BLOCK, PACKED_W), lambda b, be: (b, 0)),
    )
    return pl.pallas_call(
        _expert_body,
        grid_spec=grid_spec,
        out_shape=jax.ShapeDtypeStruct((rows, PACKED_W), jnp.uint32),
        compiler_params=_cparams(("parallel",)),
        name="moe_experts",
    )(block_e, xs, wg, wu, wd)


def _final_body(x1_ref, wt_ref, g_ref, wsg_ref, wsu_ref, wsd_ref, lg_ref, lb_ref, out_ref):
    x1 = x1_ref[...]
    xb = x1.astype(BF16)
    h = _silu(_dot(xb, wsg_ref[...])) * _dot(xb, wsu_ref[...])
    acc = DN_ALPHA * x1 + _dot(h.astype(BF16), wsd_ref[...])
    wt = wt_ref[...]
    acc_hi, acc_lo = acc[:, :PACKED_W], acc[:, PACKED_W:]
    for k in range(TOP_K):
        hi, lo = _unpack_rows(g_ref[k])
        acc_hi = acc_hi + wt[:, k:k + 1] * hi
        acc_lo = acc_lo + wt[:, k:k + 1] * lo
    acc = jnp.concatenate([acc_hi, acc_lo], axis=1)
    out_ref[...] = _layer_norm_rows(acc, lg_ref[...], lb_ref[...])


def _final(x1, wt_rows, g, wsg, wsu, wsd, lg, lb, tm=256):
    t = x1.shape[0]
    const = lambda w: pl.BlockSpec(w.shape, lambda i: (0, 0))
    return pl.pallas_call(
        _final_body,
        grid=(t // tm,),
        in_specs=[pl.BlockSpec((tm, D_MODEL), lambda i: (i, 0)),
                  pl.BlockSpec((tm, TOP_K), lambda i: (i, 0)),
                  pl.BlockSpec((TOP_K, tm, PACKED_W), lambda i: (0, i, 0)),
                  const(wsg), const(wsu), const(wsd), const(lg), const(lb)],
        out_specs=pl.BlockSpec((tm, D_MODEL), lambda i: (i, 0)),
        out_shape=jax.ShapeDtypeStruct((t, D_MODEL), F32),
        compiler_params=_cparams(("parallel",)),
        name="moe_combine_ln2",
    )(x1, wt_rows, g, wsg, wsu, wsd, lg, lb)


SC_CORES = 2
SC_SUBCORES = 16
SC_CHUNK = 128


def _sc_scatter_rows(src, dest, pad_rows, rows):
    t, w = src.shape
    workers = SC_CORES * SC_SUBCORES
    per_w = t // workers
    steps = per_w // SC_CHUNK
    pad_steps = pad_rows.shape[0] // (workers * SC_CHUNK)
    assert steps * SC_CHUNK * workers == t and pad_steps * SC_CHUNK * workers == pad_rows.shape[0]
    mesh = plsc.VectorSubcoreMesh(core_axis_name="c", subcore_axis_name="s")
    idx_tok = dest.reshape(TOP_K, workers, steps, SC_CHUNK).transpose(1, 2, 0, 3)
    idx_pad = pad_rows.reshape(workers, pad_steps, SC_CHUNK)
    zeros = jnp.zeros((SC_CHUNK, w), src.dtype)

    def body(src_hbm, tok_hbm, pad_hbm, zero_hbm, out_hbm, tok_v, pad_v, rows_v, sem):
        wid = lax.axis_index("s") * SC_CORES + lax.axis_index("c")
        pltpu.sync_copy(tok_hbm.at[wid], tok_v)
        pltpu.sync_copy(pad_hbm.at[wid], pad_v)

        @pl.loop(0, steps)
        def _(i):
            off = pl.multiple_of(wid * per_w + i * SC_CHUNK, SC_CHUNK)
            pltpu.sync_copy(src_hbm.at[pl.ds(off, SC_CHUNK)], rows_v)
            copies = [pltpu.async_copy(rows_v, out_hbm.at[tok_v.at[i, k]], sem) for k in range(TOP_K)]
            for c in copies:
                c.wait()

        pltpu.sync_copy(zero_hbm, rows_v)

        @pl.loop(0, pad_steps)
        def _(i):
            pltpu.async_copy(rows_v, out_hbm.at[pad_v.at[i]], sem).wait()

    return pl.kernel(
        body, mesh=mesh,
        out_type=jax.ShapeDtypeStruct((rows, w), src.dtype),
        scratch_types=[pltpu.VMEM((steps, TOP_K, SC_CHUNK), jnp.int32), pltpu.VMEM((pad_steps, SC_CHUNK), jnp.int32),
                       pltpu.VMEM((SC_CHUNK, w), src.dtype), pltpu.SemaphoreType.DMA],
    )(src, idx_tok, idx_pad, zeros)


def _sc_gather_rows(table, idx):
    n = idx.shape[0]
    w = table.shape[1]
    workers = SC_CORES * SC_SUBCORES
    per_w = n // workers
    steps = per_w // SC_CHUNK
    assert steps * SC_CHUNK * workers == n
    mesh = plsc.VectorSubcoreMesh(core_axis_name="c", subcore_axis_name="s")

    def body(table_hbm, idx_hbm, out_hbm, idx_v, rows_v, sem):
        wid = lax.axis_index("s") * SC_CORES + lax.axis_index("c")
        pltpu.sync_copy(idx_hbm.at[wid], idx_v)

        @pl.loop(0, steps)
        def _(i):
            off = pl.multiple_of(wid * per_w + i * SC_CHUNK, SC_CHUNK)
            pltpu.async_copy(table_hbm.at[idx_v.at[i]], rows_v, sem).wait()
            pltpu.sync_copy(rows_v, out_hbm.at[pl.ds(off, SC_CHUNK)])

    return pl.kernel(
        body, mesh=mesh,
        out_type=jax.ShapeDtypeStruct((n, w), table.dtype),
        scratch_types=[pltpu.VMEM((steps, SC_CHUNK), jnp.int32), pltpu.VMEM((SC_CHUNK, w), table.dtype),
                       pltpu.SemaphoreType.DMA],
    )(table, idx.reshape(workers, steps, SC_CHUNK))


def _dest_body(ps_ref, idx_ref, pos_ref, out_ref):
    idx = idx_ref[...]
    acc = pos_ref[...]
    for e in range(N_EXPERTS):
        acc = acc + jnp.where(idx == e, ps_ref[e], 0)
    out_ref[...] = acc


def _dest_rows(pad_start, idx, pos, tn=2048):
    t = idx.shape[1]
    tn = min(tn, t)
    tok = pl.BlockSpec((TOP_K, tn), lambda i, ps: (0, i))
    grid_spec = pltpu.PrefetchScalarGridSpec(
        num_scalar_prefetch=1, grid=(t // tn,), in_specs=[tok, tok], out_specs=tok)
    return pl.pallas_call(
        _dest_body,
        grid_spec=grid_spec,
        out_shape=jax.ShapeDtypeStruct((TOP_K, t), jnp.int32),
        compiler_params=_cparams(("parallel",)),
        name="moe_dest",
    )(pad_start, idx, pos)


def _rot_cols(w):
    half = w.shape[-1] // 2
    return jnp.concatenate([-w[..., half:], w[..., :half]], axis=-1)


def _prepare(p, seq_lens):
    f = {}
    w_in = p["w_in"]
    o = np.cumsum((0, Q_LORA, KV_LORA, QK_ROPE, D_INNER, CONV_DIM, SSD_HEADS, SSD_HEADS, D_MODEL, D_MODEL))
    seg = lambda i: w_in[:, o[i]:o[i + 1]]
    zeros = lambda n: jnp.zeros((D_MODEL, n), F32)
    f["wq"] = jnp.concatenate([seg(0), seg(1), seg(2), _rot_cols(seg(2)), zeros(QKV_W - KROPE_OFF - 2 * QK_ROPE)],
                              axis=1).astype(BF16)
    f["wz"] = seg(3).astype(BF16)
    f["wx"] = seg(4).astype(BF16)
    f["wdt"] = jnp.concatenate([seg(5), seg(6), zeros(LANES - 2 * SSD_HEADS)], axis=1).astype(BF16)
    f["wg"] = jnp.concatenate([seg(7), seg(8)], axis=1).astype(BF16)

    f["qn"] = p["q_norm_w"].reshape(1, Q_LORA)
    f["kvn"] = p["kv_norm_w"].reshape(1, KV_LORA)
    w_uq = p["w_uq"]
    nope, rope = w_uq[..., :QK_NOPE], w_uq[..., QK_NOPE:]
    padq = HEAD_PAD - QK_NOPE - QK_ROPE
    zq = lambda n: jnp.zeros((Q_LORA, N_HEADS, n), F32)
    wq_a = jnp.concatenate([nope, rope, zq(padq)], axis=-1).reshape(Q_LORA, N_HEADS * HEAD_PAD)
    wq_b = jnp.concatenate([zq(QK_NOPE), _rot_cols(rope), zq(padq)], axis=-1).reshape(Q_LORA, N_HEADS * HEAD_PAD)
    f["w_q"] = jnp.concatenate([wq_a, wq_b], axis=1).astype(BF16)
    w_ukv = p["w_ukv"]
    wk_t = jnp.transpose(w_ukv[..., :QK_NOPE], (1, 2, 0))
    eye = jnp.eye(QK_ROPE, dtype=F32)
    rope_rows = jnp.concatenate([jnp.zeros((QK_ROPE, KV_LORA), F32), eye, eye,
                                 jnp.zeros((QK_ROPE, LANES - 2 * QK_ROPE), F32)], axis=1)
    wk = jnp.concatenate([
        jnp.concatenate([wk_t, jnp.zeros((N_HEADS, QK_NOPE, LANES), F32)], axis=2),
        jnp.broadcast_to(rope_rows, (N_HEADS, QK_ROPE, KV_LORA + LANES)),
        jnp.zeros((N_HEADS, padq, KV_LORA + LANES), F32)], axis=1)
    f["w_k"] = wk.reshape(N_HEADS * HEAD_PAD, KV_LORA + LANES).astype(BF16)
    w_v = jnp.concatenate([w_ukv[..., QK_NOPE:], jnp.zeros((KV_LORA, N_HEADS, HEAD_PAD - V_DIM), F32)], axis=-1)
    f["w_v"] = w_v.reshape(KV_LORA, N_HEADS * HEAD_PAD).astype(BF16)
    f["v_one"] = (jnp.arange(N_HEADS * HEAD_PAD) % HEAD_PAD == V_DIM).astype(F32).reshape(1, -1)

    scale = (QK_NOPE + QK_ROPE) ** -0.5 * math.log2(math.e)
    f["rope"] = {}
    for s in sorted(set(seq_lens)):
        inv_freq = ROPE_THETA ** (-jnp.arange(0, QK_ROPE, 2, dtype=F32) / QK_ROPE)
        ang = jnp.arange(s, dtype=F32)[:, None] * inv_freq[None, :]
        cos, sin = jnp.cos(ang), jnp.sin(ang)
        one, zero = jnp.ones((s, QK_NOPE), F32), jnp.zeros((s, QK_NOPE), F32)
        zp = jnp.zeros((s, padq), F32)
        cq_t = scale * jnp.concatenate([one, cos, cos, zp], axis=1)
        sq_t = scale * jnp.concatenate([zero, sin, sin, zp], axis=1)
        tk_t = jnp.concatenate([cos, cos, sin, sin, jnp.zeros((s, LANES - 2 * QK_ROPE), F32)], axis=1)
        f["rope"][s] = (cq_t, sq_t, tk_t)

    f["conv_w"] = jnp.concatenate([p["conv_w"], jnp.zeros((8 - CONV_WIDTH, CONV_DIM), F32)], axis=0)
    f["conv_b"] = p["conv_b"].reshape(1, CONV_DIM)
    pad_h = jnp.zeros((LANES - 2 * SSD_HEADS,), F32)
    f["dt_bias"] = jnp.concatenate([p["dt_bias_f"], p["dt_bias_b"], pad_h]).reshape(1, LANES)
    f["a_log"] = jnp.concatenate([p["a_log_f"], p["a_log_b"], pad_h]).reshape(1, LANES)
    head_of_lane = jnp.arange(D_INNER, dtype=jnp.int32) // SSD_HEADDIM
    lane_id = jnp.arange(LANES, dtype=jnp.int32)[:, None]
    f["expand_f"] = (lane_id == head_of_lane[None, :]).astype(BF16)
    f["expand_b"] = (lane_id == head_of_lane[None, :] + SSD_HEADS).astype(BF16)
    f["d_skip"] = jnp.repeat(p["d_skip"], SSD_HEADDIM).reshape(1, D_INNER)
    f["ssd_nw"] = p["ssd_norm_w"].reshape(1, D_INNER)
    f["w_oa"] = p["w_o_attn"].astype(BF16)
    f["w_os"] = p["w_o_ssd"].astype(BF16)
    f["w_out"] = p["w_out"].astype(BF16)
    f["ln1_g"] = p["ln1_g"].reshape(1, D_MODEL)
    f["ln1_b"] = p["ln1_b"].reshape(1, D_MODEL)

    wr_t = p["w_router"].T
    wr_hi = wr_t.astype(BF16)
    f["wr_hi"] = wr_hi
    f["wr_lo"] = (wr_t - wr_hi.astype(F32)).astype(BF16)
    f["r_bias"] = p["router_bias"].reshape(N_EXPERTS, 1)
    f["w_gate"] = p["w_gate"].astype(BF16)
    f["w_up"] = p["w_up"].astype(BF16)
    f["w_down"] = p["w_down"].astype(BF16)
    f["ws_gate"] = p["ws_gate"].astype(BF16)
    f["ws_up"] = p["ws_up"].astype(BF16)
    f["ws_down"] = p["ws_down"].astype(BF16)
    f["ln2_g"] = p["ln2_g"].reshape(1, D_MODEL)
    f["ln2_b"] = p["ln2_b"].reshape(1, D_MODEL)
    return f


ROUTER_TILE = 512


def _layer(x, f):
    b, s, d = x.shape
    t = b * s
    x2d = x.reshape(t, d)
    qkv, z, xbc, dt, g = _in_proj(x2d, f["wq"], f["wz"], f["wx"], f["wdt"], f["wg"])

    cq_t, sq_t, tk_t = f["rope"][s]
    q, kt, v = _mla_prep(qkv.reshape(b, s, QKV_W), f["qn"], f["kvn"], f["w_q"], f["w_k"], f["w_v"], f["v_one"],
                         cq_t, sq_t, tk_t)
    o = _attention(q, kt, v)

    conv_out = _conv(xbc.reshape(b, s, CONV_DIM), f["conv_w"], f["conv_b"])
    dt3 = dt.reshape(b, s, LANES)
    y_f = _ssd(conv_out, dt3, f["dt_bias"], f["a_log"], f["expand_f"], reverse=False)
    y_b = _ssd(conv_out, dt3, f["dt_bias"], f["a_log"], f["expand_b"], reverse=True)

    x1, x1p = _mix(x2d, o.reshape(t, -1), y_f.reshape(t, -1), y_b.reshape(t, -1), conv_out.reshape(t, -1), z, g,
              f["w_oa"], f["w_os"], f["w_out"], f["d_skip"], f["ssd_nw"], f["ln1_g"], f["ln1_b"])

    su = (jnp.arange(ROUTER_TILE)[:, None] < jnp.arange(ROUTER_TILE)[None, :]).astype(BF16)
    idx, wts, pos, cnt = _router(x1, f["wr_hi"], f["wr_lo"], f["r_bias"], su, tm=ROUTER_TILE)

    counts = cnt[:, 0].astype(jnp.int32)
    padded = (counts + EXPERT_BLOCK - 1) // EXPERT_BLOCK * EXPERT_BLOCK
    pad_end = jnp.cumsum(padded)
    pad_start = pad_end - padded
    nb = t * TOP_K // EXPERT_BLOCK + N_EXPERTS
    n_used = pad_end[-1] // EXPERT_BLOCK
    blk_row = jnp.minimum(jnp.arange(nb, dtype=jnp.int32), n_used - 1) * EXPERT_BLOCK
    block_e = jnp.minimum(jnp.sum((blk_row[:, None] >= pad_end[None, :]).astype(jnp.int32), axis=1), N_EXPERTS - 1)

    seg_len = jnp.concatenate([padded - counts, (nb * EXPERT_BLOCK - pad_end[-1]).reshape(1)])
    seg_first = jnp.concatenate([pad_start + counts, pad_end[-1:]])
    seg_end = jnp.cumsum(seg_len)
    j = jnp.arange(N_EXPERTS * EXPERT_BLOCK, dtype=jnp.int32)
    onehot = (jnp.sum((j[:, None] >= seg_end[None, :]).astype(jnp.int32), axis=1)[:, None]
              == jnp.arange(N_EXPERTS + 1, dtype=jnp.int32)[None, :])
    pad_rows = j + jnp.sum(jnp.where(onehot, (seg_first - (seg_end - seg_len))[None, :], 0), axis=1)

    dest = _dest_rows(pad_start, idx, pos)
    xs = _sc_scatter_rows(x1p, dest, pad_rows.astype(jnp.int32), nb * EXPERT_BLOCK)
    ys = _experts(block_e, xs, f["w_gate"], f["w_up"], f["w_down"])
    g = _sc_gather_rows(ys, dest.reshape(TOP_K * t)).reshape(TOP_K, t, PACKED_W)
    y = _final(x1, wts.T, g, f["ws_gate"], f["ws_up"], f["ws_down"], f["ln2_g"], f["ln2_b"])
    return y.reshape(b, s, d)


def kernel(x_prompt, x_sample, w_in, q_norm_w, kv_norm_w, w_uq, w_ukv, w_o_attn, conv_w, conv_b, dt_bias_f,
           dt_bias_b, a_log_f, a_log_b, d_skip, ssd_norm_w, w_o_ssd, w_out, ln1_g, ln1_b, w_router, router_bias,
           w_gate, w_up, w_down, ws_gate, ws_up, ws_down, ln2_g, ln2_b):
    params = dict(w_in=w_in, q_norm_w=q_norm_w, kv_norm_w=kv_norm_w, w_uq=w_uq, w_ukv=w_ukv, w_o_attn=w_o_attn,
                  conv_w=conv_w, conv_b=conv_b, dt_bias_f=dt_bias_f, dt_bias_b=dt_bias_b, a_log_f=a_log_f,
                  a_log_b=a_log_b, d_skip=d_skip, ssd_norm_w=ssd_norm_w, w_o_ssd=w_o_ssd, w_out=w_out,
                  ln1_g=ln1_g, ln1_b=ln1_b, w_router=w_router, router_bias=router_bias, w_gate=w_gate,
                  w_up=w_up, w_down=w_down, ws_gate=ws_gate, ws_up=ws_up, ws_down=ws_down, ln2_g=ln2_g,
                  ln2_b=ln2_b)
    assert w_in.shape[0] == DEPTH == 1
    params = {k: v[0] for k, v in params.items()}
    f = _prepare(params, (x_prompt.shape[1], x_sample.shape[1]))
    return (_layer(x_prompt, f), _layer(x_sample, f))
```
